```python
import functools
import jax, jax.numpy as jnp
from jax import lax
import numpy as np

D_MODEL = 2048
BATCH = 1
SEQ = 16384
DEPTH = 1
DEC_BATCH = 32
DEC_SEQ = 4
PAST_LEN = 16384
PAGE_SIZE = 128

N_RET_HEADS = 8
RET_HEAD_DIM = D_MODEL // 16
N_ATT_HEADS = 8
ATT_HEAD_DIM = D_MODEL // 16
RET_WIDTH = N_RET_HEADS * RET_HEAD_DIM
ATT_WIDTH = N_ATT_HEADS * ATT_HEAD_DIM
MIX_WIDTH = RET_WIDTH + ATT_WIDTH
IN_WIDTH = 4 * RET_WIDTH + 3 * ATT_WIDTH
D_FF = ((8 * D_MODEL // 3 + 255) // 256) * 256
MOBA_BLOCK = 256
MOBA_TOPK = 3
RET_CHUNK = 128
Q_BLOCK = 128
N_SUB = 3
LN_EPS = 1e-5
DEEPNORM_ALPHA = (2 * DEPTH) ** 0.25
DEEPNORM_BETA = (8 * DEPTH) ** -0.25
FFN_RESIDUAL = 0.5

kernel_name = 'hybrid_retention_moba_macaron_decode_step'


def _layernorm(x, g, b):
    xf = x.astype(jnp.float32)
    mu = jnp.mean(xf, -1, keepdims=True)
    xc = xf - mu
    var = jnp.mean(xc * xc, -1, keepdims=True)
    return (xc * lax.rsqrt(var + LN_EPS) * g.astype(jnp.float32) + b.astype(jnp.float32)).astype(x.dtype)


def _swiglu(h, w_in, w_out):
    a, u = jnp.split(h @ w_in, 2, axis=-1)
    return (jax.nn.silu(a) * u) @ w_out


def _ret_log_gamma():
    return jnp.log1p(-jnp.exp2(-5.0 - jnp.arange(N_RET_HEADS, dtype=jnp.float32)))


def _alibi_slopes():
    return jnp.exp2(-8.0 * (jnp.arange(N_ATT_HEADS, dtype=jnp.float32) + 1.0) / N_ATT_HEADS)


def _retention_chunk(state, q, k, v, log_gamma):
    c = q.shape[1]
    qf = q.astype(jnp.float32)
    kf = k.astype(jnp.float32) * (RET_HEAD_DIM ** -0.5)
    vf = v.astype(jnp.float32)
    i = jnp.arange(c, dtype=jnp.float32)
    diff = i[:, None] - i[None, :]
    decay = jnp.where(diff >= 0, jnp.exp(log_gamma[:, None, None] * jnp.maximum(diff, 0.0)), 0.0)
    scores = jnp.einsum('bihd,bjhd->bhij', qf, kf) * decay
    o_inner = jnp.einsum('bhij,bjhe->bihe', scores, vf)
    cross_decay = jnp.exp(log_gamma[None, :] * (i[:, None] + 1.0))
    o_cross = jnp.einsum('bihd,bhde->bihe', qf, state) * cross_decay[None, :, :, None]
    w_tail = jnp.exp(log_gamma[None, :] * (c - 1.0 - i[:, None]))
    state_new = (jnp.exp(log_gamma * c)[None, :, None, None] * state
                 + jnp.einsum('bjhd,bjhe->bhde', kf * w_tail[None, :, :, None], vf))
    return o_inner + o_cross, state_new


def _retention_prompt(q, k, v):
    b, s, h, d = q.shape
    nc = s // RET_CHUNK
    log_gamma = _ret_log_gamma()

    def to_chunks(t):
        return t.reshape(b, nc, RET_CHUNK, h, t.shape[-1]).swapaxes(0, 1)

    def step(state, qkv):
        o, state = _retention_chunk(state, qkv[0], qkv[1], qkv[2], log_gamma)
        return state, o

    state0 = jnp.zeros((b, h, d, v.shape[-1]), jnp.float32)
    state, o = lax.scan(step, state0, (to_chunks(q), to_chunks(k), to_chunks(v)))
    return o.swapaxes(0, 1).reshape(b, s, h, v.shape[-1]), state


def _moba_select(q, kmean, cand_valid, top):
    s = jnp.einsum('bthd,bnhd->bthn', q.astype(jnp.float32), kmean.astype(jnp.float32))
    s = jnp.where(cand_valid, s, -jnp.inf)
    vals, idx = lax.top_k(s, top)
    return idx, jnp.isfinite(vals)


def _moba_attend(q, pos_q, k_own, v_own, pos_own, slopes, k_sel=None, v_sel=None, pos_sel=None, valid_sel=None):
    qf = q.astype(jnp.float32) * (ATT_HEAD_DIM ** -0.5)
    pq = pos_q.astype(jnp.float32)
    d_own = pq[:, None] - pos_own.astype(jnp.float32)[None, :]
    s_own = (jnp.einsum('bthd,blhd->bthl', qf, k_own.astype(jnp.float32))
             - slopes[None, None, :, None] * d_own[None, :, None, :])
    s_own = jnp.where((d_own >= 0)[None, :, None, :], s_own, -jnp.inf)
    if k_sel is None:
        p = jax.nn.softmax(s_own, axis=-1)
        return jnp.einsum('bthl,blhd->bthd', p, v_own.astype(jnp.float32)).astype(q.dtype)
    n_sel = k_sel.shape[3] * k_sel.shape[4]
    d_sel = pq[None, :, None, None, None] - pos_sel.astype(jnp.float32)
    s_sel = (jnp.einsum('bthd,bthkld->bthkl', qf, k_sel.astype(jnp.float32))
             - slopes[None, None, :, None, None] * d_sel)
    s_sel = jnp.where(valid_sel[..., None], s_sel, -jnp.inf)
    s = jnp.concatenate([s_sel.reshape(s_sel.shape[0], s_sel.shape[1], s_sel.shape[2], n_sel), s_own], axis=-1)
    p = jax.nn.softmax(s, axis=-1)
    p_sel = p[..., :n_sel].reshape(s_sel.shape)
    p_own = p[..., n_sel:]
    out = (jnp.einsum('bthkl,bthkld->bthd', p_sel, v_sel.astype(jnp.float32))
           + jnp.einsum('bthl,blhd->bthd', p_own, v_own.astype(jnp.float32)))
    return out.astype(q.dtype)


def _moba_prompt(q, k, v):
    b, s, h, d = q.shape
    nb = -(-s // MOBA_BLOCK)
    pad = nb * MOBA_BLOCK - s
    kp = jnp.pad(k, ((0, 0), (0, pad), (0, 0), (0, 0)))
    vp = jnp.pad(v, ((0, 0), (0, pad), (0, 0), (0, 0)))
    kb = kp.reshape(b, nb, MOBA_BLOCK, h, d)
    kmean = jnp.mean(kb, axis=2, dtype=jnp.float32)
    kb_h = kb.transpose(0, 3, 1, 2, 4)
    vb_h = vp.reshape(b, nb, MOBA_BLOCK, h, d).transpose(0, 3, 1, 2, 4)
    slopes = _alibi_slopes()
    top = min(MOBA_TOPK, nb)
    n_qb = s // Q_BLOCK
    qb = q.reshape(b, n_qb, Q_BLOCK, h, d).swapaxes(0, 1)
    bi = jnp.arange(b)[:, None, None, None]
    hi = jnp.arange(h)[None, None, :, None]
    blk = jnp.arange(nb)

    def one_block(args):
        qb_idx, qi = args
        start = qb_idx * Q_BLOCK
        own = start // MOBA_BLOCK
        pos_q = start + jnp.arange(Q_BLOCK)
        idx, valid = _moba_select(qi, kmean, blk < own, top)
        k_sel = kb_h[bi, hi, idx]
        v_sel = vb_h[bi, hi, idx]
        pos_sel = idx[..., None] * MOBA_BLOCK + jnp.arange(MOBA_BLOCK)
        own_start = own * MOBA_BLOCK
        k_own = lax.dynamic_slice_in_dim(kp, own_start, MOBA_BLOCK, axis=1)
        v_own = lax.dynamic_slice_in_dim(vp, own_start, MOBA_BLOCK, axis=1)
        pos_own = own_start + jnp.arange(MOBA_BLOCK)
        return _moba_attend(qi, pos_q, k_own, v_own, pos_own, slopes, k_sel, v_sel, pos_sel, valid)

    o = lax.map(one_block, (jnp.arange(n_qb), qb))
    return o.swapaxes(0, 1).reshape(b, s, h, d)


def _moba_sample(q, k_new, v_new, cache_k, cache_v, page_table):
    bd, t, h, d = q.shape
    n_pages = page_table.shape[1]
    past = n_pages * PAGE_SIZE
    ppb = MOBA_BLOCK // PAGE_SIZE
    n_full = past // MOBA_BLOCK
    slopes = _alibi_slopes()
    pos_q = past + jnp.arange(t)
    own_pages = page_table[:, n_full * ppb:]
    n_own_past = own_pages.shape[1] * PAGE_SIZE
    k_own = jnp.concatenate([cache_k[own_pages].reshape(bd, n_own_past, h, d), k_new], axis=1)
    v_own = jnp.concatenate([cache_v[own_pages].reshape(bd, n_own_past, h, d), v_new], axis=1)
    pos_own = n_full * MOBA_BLOCK + jnp.arange(k_own.shape[1])
    if n_full == 0:
        return _moba_attend(q, pos_q, k_own, v_own, pos_own, slopes)
    top = min(MOBA_TOPK, n_full)
    full_pages = page_table[:, :n_full * ppb]
    kmean = jnp.mean(cache_k[full_pages].reshape(bd, n_full, MOBA_BLOCK, h, d), axis=2, dtype=jnp.float32)
    idx, valid = _moba_select(q, kmean, jnp.ones((n_full,), bool), top)
    logical = idx[..., None] * ppb + jnp.arange(ppb)
    phys = page_table[jnp.arange(bd)[:, None, None, None, None], logical]
    hi = jnp.arange(h)[None, None, :, None, None]
    k_sel = cache_k[phys, :, hi].reshape(bd, t, h, top, MOBA_BLOCK, d)
    v_sel = cache_v[phys, :, hi].reshape(bd, t, h, top, MOBA_BLOCK, d)
    pos_sel = idx[..., None] * MOBA_BLOCK + jnp.arange(MOBA_BLOCK)
    return _moba_attend(q, pos_q, k_own, v_own, pos_own, slopes, k_sel, v_sel, pos_sel, valid)


def _split_heads(z):
    b, t, _ = z.shape
    cuts = [RET_WIDTH, 2 * RET_WIDTH, 3 * RET_WIDTH, 4 * RET_WIDTH,
            4 * RET_WIDTH + ATT_WIDTH, 4 * RET_WIDTH + 2 * ATT_WIDTH]
    qr, kr, vr, gr, qa, ka, va = jnp.split(z, cuts, axis=-1)
    rh = lambda u: u.reshape(b, t, N_RET_HEADS, RET_HEAD_DIM)
    ah = lambda u: u.reshape(b, t, N_ATT_HEADS, ATT_HEAD_DIM)
    return rh(qr), rh(kr), rh(vr), gr, ah(qa), ah(ka), ah(va)


def _merge_heads(o_ret, g_ret, o_att, w_o):
    b, t = o_ret.shape[:2]
    of = o_ret.astype(jnp.float32)
    mu = jnp.mean(of, -1, keepdims=True)
    var = jnp.mean(jnp.square(of - mu), -1, keepdims=True)
    on = ((of - mu) * lax.rsqrt(var + LN_EPS)).reshape(b, t, RET_WIDTH)
    ret = (jax.nn.silu(g_ret.astype(jnp.float32)) * on).astype(g_ret.dtype)
    att = o_att.reshape(b, t, ATT_WIDTH).astype(g_ret.dtype)
    return jnp.concatenate([ret, att], axis=-1) @ w_o


def _adaln(c, w_ada, b_ada):
    return (jax.nn.silu(c) @ w_ada + b_ada).reshape(c.shape[0], N_SUB, 3, D_MODEL)


def _modulate(x, mod, i):
    return x * (1.0 + mod[:, i, 1][:, None, :]) + mod[:, i, 0][:, None, :]


def _post_residual(x, y, mod, i, r, ln_g, ln_b):
    return _layernorm(DEEPNORM_ALPHA * x + r * mod[:, i, 2][:, None, :] * y, ln_g[i], ln_b[i])


def _layer(x, c, mixer, w_ada, b_ada, w_ffn1_in, w_ffn1_out, w_ffn2_in, w_ffn2_out, w_in, w_o, ln_g, ln_b):
    mod = _adaln(c, w_ada, b_ada)
    x = _post_residual(x, _swiglu(_modulate(x, mod, 0), w_ffn1_in, w_ffn1_out), mod, 0, FFN_RESIDUAL, ln_g, ln_b)
    qr, kr, vr, gr, qa, ka, va = _split_heads(_modulate(x, mod, 1) @ w_in)
    o_ret, o_att, new_state = mixer(qr, kr, vr, qa, ka, va)
    x = _post_residual(x, _merge_heads(o_ret, gr, o_att, w_o), mod, 1, 1.0, ln_g, ln_b)
    x = _post_residual(x, _swiglu(_modulate(x, mod, 2), w_ffn2_in, w_ffn2_out), mod, 2, FFN_RESIDUAL, ln_g, ln_b)
    return x, new_state


def _prompt_mixer(qr, kr, vr, qa, ka, va):
    o_ret, s_ret = _retention_prompt(qr, kr, vr)
    o_att = _moba_prompt(qa, ka, va)
    return o_ret, o_att, (ka, va, s_ret.astype(qa.dtype))


def _sample_mixer(cache_k, cache_v, state_ret, page_table, qr, kr, vr, qa, ka, va):
    o_ret, s_ret = _retention_chunk(state_ret.astype(jnp.float32), qr, kr, vr, _ret_log_gamma())
    o_att = _moba_sample(qa, ka, va, cache_k, cache_v, page_table)
    return o_ret, o_att, (ka, va, s_ret.astype(state_ret.dtype))


def setup_inputs(seed: int = 0) -> dict:
    key = jax.random.key(seed)
    ks = jax.random.split(key, 20)
    n_pages = PAST_LEN // PAGE_SIZE
    n_pool = (5 * DEC_BATCH * n_pages + 3) // 4

    def nrm(k, shape, s):
        return jax.random.normal(k, shape, jnp.float32) * s

    page_table = jax.random.permutation(ks[5], n_pool)[:DEC_BATCH * n_pages].reshape(DEC_BATCH, n_pages).astype(jnp.int32)
    return {
        'x_prompt': nrm(ks[0], (BATCH, SEQ, D_MODEL), 1.0),
        'x_sample': nrm(ks[1], (DEC_BATCH, DEC_SEQ, D_MODEL), 1.0),
        'cache_k': nrm(ks[2], (DEPTH, n_pool, PAGE_SIZE, N_ATT_HEADS, ATT_HEAD_DIM), 1.0),
        'cache_v': nrm(ks[3], (DEPTH, n_pool, PAGE_SIZE, N_ATT_HEADS, ATT_HEAD_DIM), 1.0),
        'state_ret': nrm(ks[4], (DEPTH, DEC_BATCH, N_RET_HEADS, RET_HEAD_DIM, RET_HEAD_DIM), 1.0),
        'page_table': page_table,
        'c_prompt': nrm(ks[6], (BATCH, D_MODEL), 1.0),
        'c_sample': nrm(ks[7], (DEC_BATCH, D_MODEL), 1.0),
        'w_ada': nrm(ks[8], (DEPTH, D_MODEL, N_SUB * 3 * D_MODEL), 0.5 * D_MODEL ** -0.5),
        'b_ada': nrm(ks[9], (DEPTH, N_SUB * 3 * D_MODEL), 0.02),
        'w_ffn1_in': nrm(ks[10], (DEPTH, D_MODEL, 2 * D_FF), D_MODEL ** -0.5),
        'w_ffn1_out': nrm(ks[11], (DEPTH, D_FF, D_MODEL), DEEPNORM_BETA * D_FF ** -0.5),
        'w_ffn2_in': nrm(ks[12], (DEPTH, D_MODEL, 2 * D_FF), D_MODEL ** -0.5),
        'w_ffn2_out': nrm(ks[13], (DEPTH, D_FF, D_MODEL), DEEPNORM_BETA * D_FF ** -0.5),
        'w_in': nrm(ks[14], (DEPTH, D_MODEL, IN_WIDTH), D_MODEL ** -0.5),
        'w_o': nrm(ks[15], (DEPTH, MIX_WIDTH, D_MODEL), DEEPNORM_BETA * MIX_WIDTH ** -0.5),
        'ln_g': 1.0 + nrm(ks[16], (DEPTH, N_SUB, D_MODEL), 0.02),
        'ln_b': nrm(ks[17], (DEPTH, N_SUB, D_MODEL), 0.02),
    }


def reference(x_prompt, x_sample, cache_k, cache_v, state_ret, page_table, c_prompt, c_sample,
              w_ada, b_ada, w_ffn1_in, w_ffn1_out, w_ffn2_in, w_ffn2_out, w_in, w_o, ln_g, ln_b):
    xp, xs = x_prompt, x_sample
    kp_l, vp_l, sp_l, ks_l, vs_l, ss_l = [], [], [], [], [], []
    for l in range(DEPTH):
        weights = (w_ada[l], b_ada[l], w_ffn1_in[l], w_ffn1_out[l], w_ffn2_in[l], w_ffn2_out[l],
                   w_in[l], w_o[l], ln_g[l], ln_b[l])
        xp, (kp, vp, sp) = _layer(xp, c_prompt, _prompt_mixer, *weights)
        sample_mixer = functools.partial(_sample_mixer, cache_k[l], cache_v[l], state_ret[l], page_table)
        xs, (ks, vs, ss) = _layer(xs, c_sample, sample_mixer, *weights)
        kp_l.append(kp)
        vp_l.append(vp)
        sp_l.append(sp)
        ks_l.append(ks)
        vs_l.append(vs)
        ss_l.append(ss)
    return (xp, xs, jnp.stack(kp_l), jnp.stack(vp_l), jnp.stack(sp_l), jnp.stack(ks_l), jnp.stack(vs_l), jnp.stack(ss_l))
```

```python
import functools

import jax
import jax.numpy as jnp
from jax import lax
from jax.experimental import pallas as pl
from jax.experimental.pallas import tpu as pltpu

F32 = jnp.float32
BF16 = jnp.bfloat16

D_MODEL = 2048
N_HEADS = 8
HEAD_DIM = 128
WIDTH = N_HEADS * HEAD_DIM
N_CHUNKS = 7
IN_WIDTH = N_CHUNKS * WIDTH
D_FF = 5632
MOBA_BLOCK = 256
MOBA_TOPK = 3
PAGE_SIZE = 128
PAGES_PER_BLOCK = MOBA_BLOCK // PAGE_SIZE
LN_EPS = 1e-5
DEPTH = 1
ALPHA = (2 * DEPTH) ** 0.25
FFN_RESIDUAL = 0.5
QK_SCALE = HEAD_DIM ** -0.5
NEG_INF = float("-inf")
SUBLANES = 8
LANES = 128

VMEM_LIMIT_BYTES = 56 * 1024 * 1024

CH_QR, CH_KR, CH_VR, CH_GR, CH_QA, CH_KA, CH_VA = range(N_CHUNKS)


def _params(*sem):
    return pltpu.CompilerParams(dimension_semantics=sem, vmem_limit_bytes=VMEM_LIMIT_BYTES)


def _silu(x):
    return x * jax.nn.sigmoid(x)


def _layernorm(r, g, b):
    mu = jnp.mean(r, axis=-1, keepdims=True)
    xc = r - mu
    var = jnp.mean(xc * xc, axis=-1, keepdims=True)
    return xc * lax.rsqrt(var + LN_EPS) * g + b


def _dot_nt(a, b):
    return lax.dot_general(a, b, (((1,), (1,)), ((), ())), preferred_element_type=F32)


def _dot_tn(a, b):
    return lax.dot_general(a, b, (((0,), (0,)), ((), ())), preferred_element_type=F32)


def _adaln_kernel(c_ref, w_ref, b_ref, o_ref):
    a = _silu(c_ref[...]).astype(BF16)
    o_ref[...] = jnp.dot(a, w_ref[...].astype(BF16), preferred_element_type=F32) + b_ref[...]


def _adaln(c_all, w_ada, b_ada):
    rows = c_all.shape[0]
    n = w_ada.shape[1]
    tn = 1024
    return pl.pallas_call(
        _adaln_kernel,
        grid=(n // tn,),
        in_specs=[
            pl.BlockSpec((rows, D_MODEL), lambda j: (0, 0)),
            pl.BlockSpec((D_MODEL, tn), lambda j: (0, j)),
            pl.BlockSpec((1, tn), lambda j: (0, j)),
        ],
        out_specs=pl.BlockSpec((rows, tn), lambda j: (0, j)),
        out_shape=jax.ShapeDtypeStruct((rows, n), F32),
        compiler_params=_params("arbitrary"),
        name="adaln",
    )(c_all, w_ada, b_ada.reshape(1, n))


def _mod_spec(mod, tm, sub, part):
    col = sub * 3 + part
    if mod.shape[0] == 1:
        return pl.BlockSpec((1, D_MODEL), lambda i, *_: (0, col))
    return pl.BlockSpec((tm, D_MODEL), lambda i, *_: (i, col))


def _ffn_kernel(x_ref, shift_ref, scale_ref, gate_ref, wa_ref, wu_ref, wo_ref, g_ref, b_ref,
                o_ref, xm_ref, acc_ref):
    f = pl.program_id(1)

    @pl.when(f == 0)
    def _():
        xm_ref[...] = (x_ref[...] * (1.0 + scale_ref[...]) + shift_ref[...]).astype(BF16)
        acc_ref[...] = jnp.zeros_like(acc_ref)

    xm = xm_ref[...]
    a = jnp.dot(xm, wa_ref[...], preferred_element_type=F32)
    u = jnp.dot(xm, wu_ref[...], preferred_element_type=F32)
    h = (_silu(a) * u).astype(BF16)
    acc_ref[...] += jnp.dot(h, wo_ref[...], preferred_element_type=F32)

    @pl.when(f == pl.num_programs(1) - 1)
    def _():
        r = ALPHA * x_ref[...] + FFN_RESIDUAL * gate_ref[...] * acc_ref[...]
        o_ref[...] = _layernorm(r, g_ref[...], b_ref[...])


def _ffn(x, mod, sub, w_in, w_out, ln_g, ln_b, tm, tf):
    rows = x.shape[0]
    nf = D_FF // tf
    return pl.pallas_call(
        _ffn_kernel,
        grid=(rows // tm, nf),
        in_specs=[
            pl.BlockSpec((tm, D_MODEL), lambda i, f: (i, 0)),
            _mod_spec(mod, tm, sub, 0),
            _mod_spec(mod, tm, sub, 1),
            _mod_spec(mod, tm, sub, 2),
            pl.BlockSpec((D_MODEL, tf), lambda i, f: (0, f)),
            pl.BlockSpec((D_MODEL, tf), lambda i, f: (0, f + nf)),
            pl.BlockSpec((tf, D_MODEL), lambda i, f: (f, 0)),
            pl.BlockSpec((1, D_MODEL), lambda i, f: (0, 0)),
            pl.BlockSpec((1, D_MODEL), lambda i, f: (0, 0)),
        ],
        out_specs=pl.BlockSpec((tm, D_MODEL), lambda i, f: (i, 0)),
        out_shape=jax.ShapeDtypeStruct((rows, D_MODEL), F32),
        scratch_shapes=[pltpu.VMEM((tm, D_MODEL), BF16), pltpu.VMEM((tm, D_MODEL), F32)],
        compiler_params=_params("parallel", "arbitrary"),
        name="ffn",
    )(x, mod, mod, mod, w_in, w_in, w_out, ln_g[sub:sub + 1], ln_b[sub:sub + 1])


def _inproj_kernel(x_ref, shift_ref, scale_ref, w_ref, z_ref, g_ref, k_ref, v_ref, *rest,
                   emit_transposed):
    if emit_transposed:
        qt_ref, vt_ref, xm_ref = rest
    else:
        (xm_ref,) = rest
    j = pl.program_id(1)

    @pl.when(j == 0)
    def _():
        xm_ref[...] = (x_ref[...] * (1.0 + scale_ref[...]) + shift_ref[...]).astype(BF16)

    z = jnp.dot(xm_ref[...], w_ref[...], preferred_element_type=F32)

    def put_transposed(ref, val):
        zt = val.T.astype(BF16)
        for n in range(ref.shape[0]):
            ref[n] = zt[:, n * MOBA_BLOCK:(n + 1) * MOBA_BLOCK]

    @pl.when(j != CH_QA)
    def _():
        z_ref[...] = z.astype(z_ref.dtype)

    @pl.when(j == CH_QA)
    def _():
        zs = z * QK_SCALE
        z_ref[...] = zs.astype(z_ref.dtype)
        if emit_transposed:
            put_transposed(qt_ref, zs)

    @pl.when(j == CH_GR)
    def _():
        g_ref[...] = z

    @pl.when(j == CH_KA)
    def _():
        k_ref[...] = z

    @pl.when(j == CH_VA)
    def _():
        v_ref[...] = z
        if emit_transposed:
            put_transposed(vt_ref, z)


def _inproj(x, mod, w_in, tm, z_dtype, emit_transposed):
    rows = x.shape[0]
    out_shape = [
        jax.ShapeDtypeStruct((rows, IN_WIDTH), z_dtype),
        jax.ShapeDtypeStruct((rows, WIDTH), F32),
        jax.ShapeDtypeStruct((rows, WIDTH), F32),
        jax.ShapeDtypeStruct((rows, WIDTH), F32),
    ]
    out_specs = [
        pl.BlockSpec((tm, WIDTH), lambda i, j: (i, j)),
        pl.BlockSpec((tm, WIDTH), lambda i, j: (i, 0)),
        pl.BlockSpec((tm, WIDTH), lambda i, j: (i, 0)),
        pl.BlockSpec((tm, WIDTH), lambda i, j: (i, 0)),
    ]
    if emit_transposed:
        nb = tm // MOBA_BLOCK
        for _ in range(2):
            out_shape.append(jax.ShapeDtypeStruct((rows // MOBA_BLOCK, WIDTH, MOBA_BLOCK), BF16))
            out_specs.append(pl.BlockSpec((nb, WIDTH, MOBA_BLOCK), lambda i, j: (i, 0, 0)))
    return pl.pallas_call(
        functools.partial(_inproj_kernel, emit_transposed=emit_transposed),
        grid=(rows // tm, N_CHUNKS),
        in_specs=[
            pl.BlockSpec((tm, D_MODEL), lambda i, j: (i, 0)),
            _mod_spec(mod, tm, 1, 0),
            _mod_spec(mod, tm, 1, 1),
            pl.BlockSpec((D_MODEL, WIDTH), lambda i, j: (0, j)),
        ],
        out_specs=out_specs,
        out_shape=out_shape,
        scratch_shapes=[pltpu.VMEM((tm, D_MODEL), BF16)],
        compiler_params=_params("parallel", "arbitrary"),
        name="inproj",
    )(x, mod, mod, w_in)


def _decay_tables(lg, c, n):
    row = lax.broadcasted_iota(jnp.int32, (n, n), 0)
    col = lax.broadcasted_iota(jnp.int32, (n, n), 1)
    diff = (row - col).astype(F32)
    dmat = jnp.where(diff >= 0, jnp.exp(lg * jnp.maximum(diff, 0.0)), 0.0) * QK_SCALE
    i = lax.broadcasted_iota(jnp.int32, (n, HEAD_DIM), 0).astype(F32)
    cross = jnp.exp(lg * (i + 1.0))
    wtail = jnp.exp(lg * (c - 1.0 - i)) * QK_SCALE
    return dmat, cross, wtail


def _groupnorm_gate(o, g):
    mu = jnp.mean(o, axis=-1, keepdims=True)
    oc = o - mu
    var = jnp.mean(oc * oc, axis=-1, keepdims=True)
    return _silu(g) * (oc * lax.rsqrt(var + LN_EPS))


def _retention_chunk(q, k, kf, v, state, dmat, cross, wtail, gc):
    p = (_dot_nt(q, k) * dmat).astype(BF16)
    o = jnp.dot(p, v, preferred_element_type=F32)
    o = o + jnp.dot(q, state.astype(BF16), preferred_element_type=F32) * cross
    kw = (kf * wtail).astype(BF16)
    return o, gc * state + _dot_tn(kw, v)


def _ret_prompt_kernel(lg_ref, q_ref, k_ref, v_ref, g_ref, o_ref, state_ref,
                       dmat_ref, cross_ref, wtail_ref, *, chunk):
    h = pl.program_id(0)
    c = pl.program_id(1)
    lg = lg_ref[h]

    @pl.when(c == 0)
    def _():
        dmat, cross, wtail = _decay_tables(lg, chunk, chunk)
        dmat_ref[...] = dmat
        cross_ref[...] = cross
        wtail_ref[...] = wtail
        state_ref[...] = jnp.zeros_like(state_ref)

    k = k_ref[...]
    gc = jnp.exp(jnp.full((HEAD_DIM, HEAD_DIM), lg * chunk, F32))
    o, state = _retention_chunk(q_ref[...], k, k.astype(F32), v_ref[...], state_ref[...],
                                dmat_ref[...], cross_ref[...], wtail_ref[...], gc)
    state_ref[...] = state
    o_ref[...] = _groupnorm_gate(o, g_ref[...]).astype(o_ref.dtype)


def _ret_prompt(zb, g, log_gamma, chunk):
    s = zb.shape[0]
    return pl.pallas_call(
        functools.partial(_ret_prompt_kernel, chunk=chunk),
        grid=(N_HEADS, s // chunk),
        in_specs=[
            pl.BlockSpec(memory_space=pltpu.SMEM),
            pl.BlockSpec((chunk, HEAD_DIM), lambda h, c: (c, CH_QR * N_HEADS + h)),
            pl.BlockSpec((chunk, HEAD_DIM), lambda h, c: (c, CH_KR * N_HEADS + h)),
            pl.BlockSpec((chunk, HEAD_DIM), lambda h, c: (c, CH_VR * N_HEADS + h)),
            pl.BlockSpec((chunk, HEAD_DIM), lambda h, c: (c, h)),
        ],
        out_specs=[
            pl.BlockSpec((chunk, HEAD_DIM), lambda h, c: (c, h)),
            pl.BlockSpec((None, HEAD_DIM, HEAD_DIM), lambda h, c: (h, 0, 0)),
        ],
        out_shape=[
            jax.ShapeDtypeStruct((s, WIDTH), BF16),
            jax.ShapeDtypeStruct((N_HEADS, HEAD_DIM, HEAD_DIM), F32),
        ],
        scratch_shapes=[
            pltpu.VMEM((chunk, chunk), F32),
            pltpu.VMEM((chunk, HEAD_DIM), F32),
            pltpu.VMEM((chunk, HEAD_DIM), F32),
        ],
        compiler_params=_params("parallel", "arbitrary"),
        name="ret_prompt",
    )(log_gamma, zb, zb, zb, g)


def _ret_sample_kernel(lg_ref, z_ref, g_ref, st_ref, o_ref, sn_ref, *, t):
    n = z_ref.shape[0]
    for h in range(N_HEADS):
        lg = lg_ref[h]
        dmat, cross, wtail = _decay_tables(lg, t, n)

        def head(ch, h=h):
            c0 = (ch * N_HEADS + h) * HEAD_DIM
            return z_ref[:, c0:c0 + HEAD_DIM]

        kf = head(CH_KR)
        gc = jnp.exp(jnp.full((HEAD_DIM, HEAD_DIM), lg * t, F32))
        o, state = _retention_chunk(head(CH_QR).astype(BF16), kf.astype(BF16), kf,
                                    head(CH_VR).astype(BF16), st_ref[h], dmat, cross, wtail, gc)
        sn_ref[h] = state
        gh = g_ref[:, h * HEAD_DIM:(h + 1) * HEAD_DIM]
        o_ref[:, h * HEAD_DIM:(h + 1) * HEAD_DIM] = _groupnorm_gate(o, gh)


def _ret_sample(z3, g3, state, log_gamma, t):
    b, n, _ = z3.shape
    return pl.pallas_call(
        functools.partial(_ret_sample_kernel, t=t),
        grid=(b,),
        in_specs=[
            pl.BlockSpec(memory_space=pltpu.SMEM),
            pl.BlockSpec((None, n, IN_WIDTH), lambda i: (i, 0, 0)),
            pl.BlockSpec((None, n, WIDTH), lambda i: (i, 0, 0)),
            pl.BlockSpec((None, N_HEADS, HEAD_DIM, HEAD_DIM), lambda i: (i, 0, 0, 0)),
        ],
        out_specs=[
            pl.BlockSpec((None, n, WIDTH), lambda i: (i, 0, 0)),
            pl.BlockSpec((None, N_HEADS, HEAD_DIM, HEAD_DIM), lambda i: (i, 0, 0, 0)),
        ],
        out_shape=[
            jax.ShapeDtypeStruct((b, n, WIDTH), F32),
            jax.ShapeDtypeStruct(state.shape, F32),
        ],
        compiler_params=_params("parallel"),
        name="ret_sample",
    )(log_gamma, z3, g3, state)


def _kmean_kernel(k_ref, o_ref):
    nb = o_ref.shape[0]
    k = k_ref[...].reshape(nb, MOBA_BLOCK, WIDTH)
    o_ref[...] = jnp.sum(k, axis=1) * (1.0 / MOBA_BLOCK)


def _kmean_prompt(k):
    s = k.shape[0]
    nb = s // MOBA_BLOCK
    per = SUBLANES if nb % SUBLANES == 0 else nb
    return pl.pallas_call(
        _kmean_kernel,
        grid=(nb // per,),
        in_specs=[pl.BlockSpec((per * MOBA_BLOCK, WIDTH), lambda i: (i, 0))],
        out_specs=pl.BlockSpec((per, WIDTH), lambda i: (i, 0)),
        out_shape=jax.ShapeDtypeStruct((nb, WIDTH), F32),
        compiler_params=_params("parallel"),
        name="kmean_prompt",
    )(k)


def _top_mask_t(gate_t, n_valid, top):
    n = gate_t.shape[0]
    blk = lax.broadcasted_iota(jnp.int32, gate_t.shape, 0)
    g = jnp.where(blk < n_valid, gate_t, NEG_INF)
    sel = jnp.zeros(gate_t.shape, F32)
    for _ in range(top):
        m = jnp.max(g, axis=0, keepdims=True)
        first = jnp.min(jnp.where(g == m, blk, n), axis=0, keepdims=True)
        hit = blk == first
        finite = jnp.abs(m) < float("inf")
        sel = jnp.where(hit, jnp.where(finite, 1.0, sel), sel)
        g = jnp.where(hit, NEG_INF, g)
    return sel


def _moba_prompt_kernel(slope_ref, qt_ref, k_ref, vt_ref, km_ref, o_ref, sel_ref, bias_ref):
    h = pl.program_id(0)
    i = pl.program_id(1)
    slope = slope_ref[h]
    blk = MOBA_BLOCK
    kc = lax.broadcasted_iota(jnp.int32, (blk, blk), 0)
    qr = lax.broadcasted_iota(jnp.int32, (blk, blk), 1)

    @pl.when(i == 0)
    def _():
        bias_ref[...] = -slope * (qr - kc).astype(F32)

    qt = qt_ref[...]
    gate_t = jnp.dot(km_ref[...].astype(BF16), qt, preferred_element_type=F32)
    sel_ref[...] = _top_mask_t(gate_t, i, MOBA_TOPK)

    start = pl.multiple_of(i * blk, blk)
    s_t = jnp.dot(k_ref[pl.ds(start, blk), :], qt, preferred_element_type=F32) + bias_ref[...]
    s_t = jnp.where(kc <= qr, s_t, NEG_INF)
    m0 = jnp.max(s_t, axis=0, keepdims=True)
    p0 = jnp.exp(s_t - m0)
    l0 = jnp.sum(p0, axis=0, keepdims=True)
    acc0 = jnp.dot(vt_ref[i], p0.astype(BF16), preferred_element_type=F32)

    def body(j, carry):
        m, l, acc = carry
        js = pl.multiple_of(j * blk, blk)
        u = jnp.dot(k_ref[pl.ds(js, blk), :], qt, preferred_element_type=F32) + bias_ref[...]
        u = jnp.where(sel_ref[pl.ds(j, 1), :] > 0.0, u, NEG_INF)
        off = -slope * ((i - j) * blk).astype(F32)
        m_new = jnp.maximum(m, jnp.max(u, axis=0, keepdims=True) + off)
        pj = jnp.exp(u - (m_new - off))
        alpha = jnp.exp(m - m_new)
        l = alpha * l + jnp.sum(pj, axis=0, keepdims=True)
        acc = alpha * acc + jnp.dot(vt_ref[j], pj.astype(BF16), preferred_element_type=F32)
        return m_new, l, acc

    _, l, acc = lax.fori_loop(0, i, body, (m0, l0, acc0))
    o_ref[...] = (acc / l).T.astype(o_ref.dtype)


def _moba_prompt(zb, qt, vt, kmean, slopes):
    s = zb.shape[0]
    nb = s // MOBA_BLOCK
    return pl.pallas_call(
        _moba_prompt_kernel,
        grid=(N_HEADS, nb),
        in_specs=[
            pl.BlockSpec(memory_space=pltpu.SMEM),
            pl.BlockSpec((None, HEAD_DIM, MOBA_BLOCK), lambda h, i: (i, h, 0)),
            pl.BlockSpec((s, HEAD_DIM), lambda h, i: (0, CH_KA * N_HEADS + h)),
            pl.BlockSpec((nb, HEAD_DIM, MOBA_BLOCK), lambda h, i: (0, h, 0)),
            pl.BlockSpec((nb, HEAD_DIM), lambda h, i: (0, h)),
        ],
        out_specs=pl.BlockSpec((MOBA_BLOCK, HEAD_DIM), lambda h, i: (i, h)),
        out_shape=jax.ShapeDtypeStruct((s, WIDTH), BF16),
        scratch_shapes=[
            pltpu.VMEM((nb, MOBA_BLOCK), F32),
            pltpu.VMEM((MOBA_BLOCK, MOBA_BLOCK), F32),
        ],
        compiler_params=_params("parallel", "arbitrary"),
        name="moba_prompt",
    )(slopes, qt, zb, vt, kmean)


PAGES_PER_STEP = SUBLANES * PAGES_PER_BLOCK


def _kmean_sample_kernel(pt_ref, *refs):
    del pt_ref
    page_refs, o_ref = refs[:-1], refs[-1]
    for n in range(len(page_refs) // PAGES_PER_BLOCK):
        tot = None
        for r in range(PAGES_PER_BLOCK):
            part = jnp.sum(page_refs[n * PAGES_PER_BLOCK + r][...], axis=0)
            tot = part if tot is None else tot + part
        o_ref[n] = tot * (1.0 / MOBA_BLOCK)


def _kmean_sample(cache_k, page_table):
    b, n_pages = page_table.shape
    n_full = n_pages // PAGES_PER_BLOCK
    pps = PAGES_PER_STEP if n_pages % PAGES_PER_STEP == 0 else n_pages
    bps = pps // PAGES_PER_BLOCK

    def page_spec(r):
        return pl.BlockSpec((None, PAGE_SIZE, N_HEADS, HEAD_DIM),
                            lambda i, s, pt: (pt[i, s * pps + r], 0, 0, 0))

    grid_spec = pltpu.PrefetchScalarGridSpec(
        num_scalar_prefetch=1,
        grid=(b, n_pages // pps),
        in_specs=[page_spec(r) for r in range(pps)],
        out_specs=pl.BlockSpec((None, bps, N_HEADS, HEAD_DIM), lambda i, s, pt: (i, s, 0, 0)),
    )
    return pl.pallas_call(
        _kmean_sample_kernel,
        grid_spec=grid_spec,
        out_shape=jax.ShapeDtypeStruct((b, n_full, N_HEADS, HEAD_DIM), F32),
        compiler_params=_params("parallel", "arbitrary"),
        name="kmean_sample",
    )(page_table, *([cache_k] * pps))


def _gate_sample_kernel(z_ref, km_ref, o_ref):
    rows = z_ref.shape[0]
    n = km_ref.shape[0]
    lane = lax.broadcasted_iota(jnp.int32, (rows, LANES), 1)
    blk = lax.broadcasted_iota(jnp.int32, (rows, n), 1)
    out = jnp.zeros((rows, LANES), jnp.int32)
    for h in range(N_HEADS):
        c0 = (CH_QA * N_HEADS + h) * HEAD_DIM
        q = z_ref[:, c0:c0 + HEAD_DIM].astype(BF16)
        kmh = km_ref[:, h, :].astype(BF16)
        g = _dot_nt(q, kmh)
        for r in range(MOBA_TOPK):
            m = jnp.max(g, axis=1, keepdims=True)
            first = jnp.min(jnp.where(g == m, blk, n), axis=1, keepdims=True)
            out = jnp.where(lane == h * MOBA_TOPK + r, first, out)
            g = jnp.where(blk == first, NEG_INF, g)
    o_ref[...] = out


def _gate_sample(z3, kmean_s):
    b, rows, _ = z3.shape
    n = kmean_s.shape[1]
    return pl.pallas_call(
        _gate_sample_kernel,
        grid=(b,),
        in_specs=[
            pl.BlockSpec((None, rows, IN_WIDTH), lambda i: (i, 0, 0)),
            pl.BlockSpec((None, n, N_HEADS, HEAD_DIM), lambda i: (i, 0, 0, 0)),
        ],
        out_specs=pl.BlockSpec((None, rows, LANES), lambda i: (i, 0, 0)),
        out_shape=jax.ShapeDtypeStruct((b, rows, LANES), jnp.int32),
        compiler_params=_params("parallel"),
        name="gate_sample",
    )(z3, kmean_s)


def _attn_sample_kernel(pt_ref, idx_ref, slope_ref, q_ref, kn_ref, vn_ref, ck_hbm, cv_hbm, o_ref,
                        kbuf, vbuf, sem, *, t, past):
    per_tok = MOBA_TOPK * PAGES_PER_BLOCK
    b = pl.program_id(0)
    h = pl.program_id(1)
    step = b * N_HEADS + h
    slot = lax.rem(step, 2)

    def copies(bb, hh, sl):
        out = []
        for tt in range(t):
            for r in range(MOBA_TOPK):
                blk = idx_ref[((bb * t + tt) * N_HEADS + hh) * MOBA_TOPK + r]
                for pg in range(PAGES_PER_BLOCK):
                    phys = pt_ref[bb, blk * PAGES_PER_BLOCK + pg]
                    j = tt * per_tok + r * PAGES_PER_BLOCK + pg
                    out.append(pltpu.make_async_copy(ck_hbm.at[phys, :, hh, :], kbuf.at[sl, j],
                                                     sem.at[sl]))
                    out.append(pltpu.make_async_copy(cv_hbm.at[phys, :, hh, :], vbuf.at[sl, j],
                                                     sem.at[sl]))
        return out

    @pl.when(step == 0)
    def _():
        for cp in copies(b, h, slot):
            cp.start()

    @pl.when(step + 1 < pl.num_programs(0) * N_HEADS)
    def _():
        last_head = h == N_HEADS - 1
        for cp in copies(jnp.where(last_head, b + 1, b), jnp.where(last_head, 0, h + 1), 1 - slot):
            cp.start()

    for cp in copies(b, h, slot):
        cp.wait()

    k_refs = [kbuf.at[slot, j] for j in range(t * per_tok)]
    v_refs = [vbuf.at[slot, j] for j in range(t * per_tok)]
    slope = slope_ref[h]
    rows = q_ref.shape[0]
    q = q_ref[...].astype(BF16)
    zpad = jnp.zeros((LANES - rows, HEAD_DIM), F32)
    kn = jnp.concatenate([kn_ref[...], zpad], axis=0)
    vn = jnp.concatenate([vn_ref[...], zpad], axis=0)
    lane_blk = lax.broadcasted_iota(jnp.int32, (1, MOBA_BLOCK), 1)
    lane_new = lax.broadcasted_iota(jnp.int32, (1, LANES), 1)
    for tt in range(t):
        ks = jnp.concatenate([k_refs[tt * per_tok + r][...] for r in range(per_tok)] + [kn], axis=0)
        vs = jnp.concatenate([v_refs[tt * per_tok + r][...] for r in range(per_tok)] + [vn], axis=0)
        s = _dot_nt(q, ks.astype(BF16))[tt:tt + 1]
        dist = []
        for r in range(MOBA_TOPK):
            blk = idx_ref[((b * t + tt) * N_HEADS + h) * MOBA_TOPK + r]
            dist.append((past + tt - blk * MOBA_BLOCK - lane_blk).astype(F32))
        dist.append((tt - lane_new).astype(F32))
        valid = jnp.concatenate([jnp.ones((1, MOBA_TOPK * MOBA_BLOCK), jnp.int32),
                                 (lane_new <= tt).astype(jnp.int32)], axis=1)
        s = jnp.where(valid > 0, s - slope * jnp.concatenate(dist, axis=1), NEG_INF)
        m = jnp.max(s, axis=1, keepdims=True)
        p = jnp.exp(s - m)
        l = jnp.sum(p, axis=1, keepdims=True)
        p8 = jnp.broadcast_to(p, (SUBLANES, p.shape[1])).astype(BF16)
        o = jnp.dot(p8, vs.astype(BF16), preferred_element_type=F32)[0:1]
        o_ref[tt:tt + 1, :] = o / l
    if rows > t:
        o_ref[t:, :] = jnp.zeros((rows - t, HEAD_DIM), F32)


def _attn_sample(z3, cache_k, cache_v, page_table, idx_flat, slopes, t, past):
    b, rows, _ = z3.shape
    n_sel = t * MOBA_TOPK * PAGES_PER_BLOCK

    def tok_spec(ch):
        return pl.BlockSpec((None, rows, HEAD_DIM), lambda i, h, pt, idx: (i, 0, ch * N_HEADS + h))

    grid_spec = pltpu.PrefetchScalarGridSpec(
        num_scalar_prefetch=2,
        grid=(b, N_HEADS),
        in_specs=[pl.BlockSpec(memory_space=pltpu.SMEM), tok_spec(CH_QA), tok_spec(CH_KA),
                  tok_spec(CH_VA), pl.BlockSpec(memory_space=pl.ANY),
                  pl.BlockSpec(memory_space=pl.ANY)],
        out_specs=pl.BlockSpec((None, rows, HEAD_DIM), lambda i, h, pt, idx: (i, 0, h)),
        scratch_shapes=[
            pltpu.VMEM((2, n_sel, PAGE_SIZE, HEAD_DIM), F32),
            pltpu.VMEM((2, n_sel, PAGE_SIZE, HEAD_DIM), F32),
            pltpu.SemaphoreType.DMA((2,)),
        ],
    )
    return pl.pallas_call(
        functools.partial(_attn_sample_kernel, t=t, past=past),
        grid_spec=grid_spec,
        out_shape=jax.ShapeDtypeStruct((b, rows, WIDTH), F32),
        compiler_params=_params("arbitrary", "arbitrary"),
        name="attn_sample",
    )(page_table, idx_flat, slopes, z3, z3, z3, cache_k, cache_v)


def _outproj_kernel(x_ref, gate_ref, ret_ref, att_ref, wr_ref, wa_ref, g_ref, b_ref, o_ref):
    y = jnp.dot(ret_ref[...].astype(BF16), wr_ref[...], preferred_element_type=F32)
    y = y + jnp.dot(att_ref[...].astype(BF16), wa_ref[...], preferred_element_type=F32)
    r = ALPHA * x_ref[...] + gate_ref[...] * y
    o_ref[...] = _layernorm(r, g_ref[...], b_ref[...])


def _outproj(x, mod, ret, att, w_o, ln_g, ln_b, tm):
    rows = x.shape[0]
    return pl.pallas_call(
        _outproj_kernel,
        grid=(rows // tm,),
        in_specs=[
            pl.BlockSpec((tm, D_MODEL), lambda i: (i, 0)),
            _mod_spec(mod, tm, 1, 2),
            pl.BlockSpec((tm, WIDTH), lambda i: (i, 0)),
            pl.BlockSpec((tm, WIDTH), lambda i: (i, 0)),
            pl.BlockSpec((WIDTH, D_MODEL), lambda i: (0, 0)),
            pl.BlockSpec((WIDTH, D_MODEL), lambda i: (1, 0)),
            pl.BlockSpec((1, D_MODEL), lambda i: (0, 0)),
            pl.BlockSpec((1, D_MODEL), lambda i: (0, 0)),
        ],
        out_specs=pl.BlockSpec((tm, D_MODEL), lambda i: (i, 0)),
        out_shape=jax.ShapeDtypeStruct((rows, D_MODEL), F32),
        compiler_params=_params("parallel"),
        name="outproj",
    )(x, mod, ret, att, w_o, w_o, ln_g[1:2], ln_b[1:2])


def _row_tile(rows, want):
    return want if rows % want == 0 else rows


def kernel(x_prompt, x_sample, cache_k, cache_v, state_ret, page_table, c_prompt, c_sample,
           w_ada, b_ada, w_ffn1_in, w_ffn1_out, w_ffn2_in, w_ffn2_out, w_in, w_o, ln_g, ln_b):
    assert w_ada.shape[0] == DEPTH and x_prompt.shape[0] == 1
    seq = x_prompt.shape[1]
    bd, t, _ = x_sample.shape
    n_pages = page_table.shape[1]
    past = n_pages * PAGE_SIZE
    assert seq % MOBA_BLOCK == 0 and n_pages % PAGES_PER_BLOCK == 0
    assert n_pages // PAGES_PER_BLOCK >= MOBA_TOPK and t <= SUBLANES

    hs = jnp.arange(N_HEADS, dtype=F32)
    log_gamma = jnp.log1p(-jnp.exp2(-5.0 - hs))
    slopes = jnp.exp2(-8.0 * (hs + 1.0) / N_HEADS)

    w1i, w1o = w_ffn1_in[0].astype(BF16), w_ffn1_out[0].astype(BF16)
    w2i, w2o = w_ffn2_in[0].astype(BF16), w_ffn2_out[0].astype(BF16)
    wi, wo = w_in[0].astype(BF16), w_o[0].astype(BF16)
    g_ln, b_ln = ln_g[0], ln_b[0]

    pad = (-(1 + bd)) % SUBLANES
    c_all = jnp.concatenate([c_prompt, c_sample, jnp.zeros((pad, D_MODEL), F32)], axis=0)
    mod = _adaln(c_all, w_ada[0], b_ada[0])
    mod_p = mod[0:1]
    mod_s = jnp.repeat(mod[1:1 + bd], t, axis=0)

    xp = x_prompt[0]
    tm_p = _row_tile(seq, 512)
    tf = 512
    xp = _ffn(xp, mod_p, 0, w1i, w1o, g_ln, b_ln, tm_p, tf)
    zb, g_r, k_a, v_a, qt, vt = _inproj(xp, mod_p, wi, tm_p, BF16, True)
    ret, state_p = _ret_prompt(zb, g_r, log_gamma, MOBA_BLOCK)
    kmean_p = _kmean_prompt(k_a)
    att = _moba_prompt(zb, qt, vt, kmean_p, slopes)
    xp = _outproj(xp, mod_p, ret, att, wo, g_ln, b_ln, tm_p)
    xp = _ffn(xp, mod_p, 2, w2i, w2o, g_ln, b_ln, tm_p, tf)

    rows_s = bd * t
    xs = x_sample.reshape(rows_s, D_MODEL)
    xs = _ffn(xs, mod_s, 0, w1i, w1o, g_ln, b_ln, rows_s, tf)
    zs, gs, ks, vs = _inproj(xs, mod_s, wi, rows_s, F32, False)
    tpad = ((0, 0), (0, SUBLANES - t), (0, 0))
    zs3 = jnp.pad(zs.reshape(bd, t, IN_WIDTH), tpad)
    gs3 = jnp.pad(gs.reshape(bd, t, WIDTH), tpad)
    ret_s, state_s = _ret_sample(zs3, gs3, state_ret[0], log_gamma, t)
    kmean_s = _kmean_sample(cache_k[0], page_table)
    idx = _gate_sample(zs3, kmean_s)[:, :t, :N_HEADS * MOBA_TOPK].reshape(-1)
    att_s = _attn_sample(zs3, cache_k[0], cache_v[0], page_table, idx, slopes, t, past)
    xs = _outproj(xs, mod_s, ret_s[:, :t].reshape(rows_s, WIDTH),
                  att_s[:, :t].reshape(rows_s, WIDTH), wo, g_ln, b_ln, rows_s)
    xs = _ffn(xs, mod_s, 2, w2i, w2o, g_ln, b_ln, rows_s, tf)

    hd = (N_HEADS, HEAD_DIM)
    return (xp[None], xs.reshape(bd, t, D_MODEL),
            k_a.reshape(1, 1, seq, *hd), v_a.reshape(1, 1, seq, *hd), state_p[None, None],
            ks.reshape(1, bd, t, *hd), vs.reshape(1, bd, t, *hd), state_s[None])
```

```python
import functools

import jax
import jax.numpy as jnp
from jax import lax
from jax.experimental import pallas as pl
from jax.experimental.pallas import tpu as pltpu

F32 = jnp.float32
BF16 = jnp.bfloat16

D_MODEL = 2048
N_HEADS = 8
HEAD_DIM = 128
WIDTH = N_HEADS * HEAD_DIM
N_CHUNKS = 7
IN_WIDTH = N_CHUNKS * WIDTH
D_FF = 5632
MOBA_BLOCK = 256
MOBA_TOPK = 3
PAGE_SIZE = 128
PAGES_PER_BLOCK = MOBA_BLOCK // PAGE_SIZE
LN_EPS = 1e-5
DEPTH = 1
ALPHA = (2 * DEPTH) ** 0.25
FFN_RESIDUAL = 0.5
QK_SCALE = HEAD_DIM ** -0.5
NEG_INF = float("-inf")
SUBLANES = 8
BF16_SUBLANES = 16
LANES = 128
V_AUG_ROWS = HEAD_DIM + BF16_SUBLANES
MOBA_CHAINS = 4

VMEM_LIMIT_BYTES = 56 * 1024 * 1024

CH_QR, CH_KR, CH_VR, CH_GR, CH_QA, CH_KA, CH_VA = range(N_CHUNKS)


def _params(*sem):
    return pltpu.CompilerParams(dimension_semantics=sem, vmem_limit_bytes=VMEM_LIMIT_BYTES)


def _silu(x):
    return x * jax.nn.sigmoid(x)


def _layernorm(r, g, b):
    mu = jnp.mean(r, axis=-1, keepdims=True)
    xc = r - mu
    var = jnp.mean(xc * xc, axis=-1, keepdims=True)
    return xc * lax.rsqrt(var + LN_EPS) * g + b


def _dot_nt(a, b):
    return lax.dot_general(a, b, (((1,), (1,)), ((), ())), preferred_element_type=F32)


def _dot_tn(a, b):
    return lax.dot_general(a, b, (((0,), (0,)), ((), ())), preferred_element_type=F32)


def _adaln_kernel(c_ref, w_ref, b_ref, o_ref):
    a = _silu(c_ref[...]).astype(BF16)
    o_ref[...] = jnp.dot(a, w_ref[...].astype(BF16), preferred_element_type=F32) + b_ref[...]


def _adaln(c_all, w_ada, b_ada):
    rows = c_all.shape[0]
    n = w_ada.shape[1]
    tn = 1024
    return pl.pallas_call(
        _adaln_kernel,
        grid=(n // tn,),
        in_specs=[
            pl.BlockSpec((rows, D_MODEL), lambda j: (0, 0)),
            pl.BlockSpec((D_MODEL, tn), lambda j: (0, j)),
            pl.BlockSpec((1, tn), lambda j: (0, j)),
        ],
        out_specs=pl.BlockSpec((rows, tn), lambda j: (0, j)),
        out_shape=jax.ShapeDtypeStruct((rows, n), F32),
        compiler_params=_params("arbitrary"),
        name="adaln",
    )(c_all, w_ada, b_ada.reshape(1, n))


def _mod_spec(mod, tm, sub, part):
    col = sub * 3 + part
    if mod.shape[0] == 1:
        return pl.BlockSpec((1, D_MODEL), lambda i, *_: (0, col))
    return pl.BlockSpec((tm, D_MODEL), lambda i, *_: (i, col))


def _ffn_kernel(x_ref, shift_ref, scale_ref, gate_ref, wa_ref, wu_ref, wo_ref, g_ref, b_ref,
                o_ref, xm_ref, acc_ref):
    f = pl.program_id(1)

    @pl.when(f == 0)
    def _():
        xm_ref[...] = (x_ref[...] * (1.0 + scale_ref[...]) + shift_ref[...]).astype(BF16)
        acc_ref[...] = jnp.zeros_like(acc_ref)

    xm = xm_ref[...]
    a = jnp.dot(xm, wa_ref[...], preferred_element_type=F32)
    u = jnp.dot(xm, wu_ref[...], preferred_element_type=F32)
    h = (_silu(a) * u).astype(BF16)
    acc_ref[...] += jnp.dot(h, wo_ref[...], preferred_element_type=F32)

    @pl.when(f == pl.num_programs(1) - 1)
    def _():
        r = ALPHA * x_ref[...] + FFN_RESIDUAL * gate_ref[...] * acc_ref[...]
        o_ref[...] = _layernorm(r, g_ref[...], b_ref[...])


def _ffn(x, mod, sub, w_in, w_out, ln_g, ln_b, tm, tf):
    rows = x.shape[0]
    nf = D_FF // tf
    return pl.pallas_call(
        _ffn_kernel,
        grid=(rows // tm, nf),
        in_specs=[
            pl.BlockSpec((tm, D_MODEL), lambda i, f: (i, 0)),
            _mod_spec(mod, tm, sub, 0),
            _mod_spec(mod, tm, sub, 1),
            _mod_spec(mod, tm, sub, 2),
            pl.BlockSpec((D_MODEL, tf), lambda i, f: (0, f)),
            pl.BlockSpec((D_MODEL, tf), lambda i, f: (0, f + nf)),
            pl.BlockSpec((tf, D_MODEL), lambda i, f: (f, 0)),
            pl.BlockSpec((1, D_MODEL), lambda i, f: (0, 0)),
            pl.BlockSpec((1, D_MODEL), lambda i, f: (0, 0)),
        ],
        out_specs=pl.BlockSpec((tm, D_MODEL), lambda i, f: (i, 0)),
        out_shape=jax.ShapeDtypeStruct((rows, D_MODEL), F32),
        scratch_shapes=[pltpu.VMEM((tm, D_MODEL), BF16), pltpu.VMEM((tm, D_MODEL), F32)],
        compiler_params=_params("parallel", "arbitrary"),
        name="ffn",
    )(x, mod, mod, mod, w_in, w_in, w_out, ln_g[sub:sub + 1], ln_b[sub:sub + 1])


def _inproj_kernel(x_ref, shift_ref, scale_ref, w_ref, z_ref, g_ref, k_ref, v_ref, *rest,
                   emit_transposed):
    if emit_transposed:
        qt_ref, vt_ref, xm_ref = rest
    else:
        (xm_ref,) = rest
    j = pl.program_id(1)

    @pl.when(j == 0)
    def _():
        xm_ref[...] = (x_ref[...] * (1.0 + scale_ref[...]) + shift_ref[...]).astype(BF16)

    z = jnp.dot(xm_ref[...], w_ref[...], preferred_element_type=F32)

    def put_transposed(ref, val):
        zt = val.T.astype(BF16)
        for n in range(ref.shape[0]):
            blk = zt[:, n * MOBA_BLOCK:(n + 1) * MOBA_BLOCK]
            if ref.ndim == 3:
                ref[n] = blk
            else:
                for h in range(N_HEADS):
                    ref[n, h, :HEAD_DIM, :] = blk[h * HEAD_DIM:(h + 1) * HEAD_DIM]
                    ref[n, h, HEAD_DIM:, :] = jnp.ones((V_AUG_ROWS - HEAD_DIM, MOBA_BLOCK), BF16)

    @pl.when(j != CH_QA)
    def _():
        z_ref[...] = z.astype(z_ref.dtype)

    @pl.when(j == CH_QA)
    def _():
        zs = z * QK_SCALE
        z_ref[...] = zs.astype(z_ref.dtype)
        if emit_transposed:
            put_transposed(qt_ref, zs)

    @pl.when(j == CH_GR)
    def _():
        g_ref[...] = z

    @pl.when(j == CH_KA)
    def _():
        k_ref[...] = z

    @pl.when(j == CH_VA)
    def _():
        v_ref[...] = z
        if emit_transposed:
            put_transposed(vt_ref, z)


def _inproj(x, mod, w_in, tm, z_dtype, emit_transposed):
    rows = x.shape[0]
    out_shape = [
        jax.ShapeDtypeStruct((rows, IN_WIDTH), z_dtype),
        jax.ShapeDtypeStruct((rows, WIDTH), F32),
        jax.ShapeDtypeStruct((rows, WIDTH), F32),
        jax.ShapeDtypeStruct((rows, WIDTH), F32),
    ]
    out_specs = [
        pl.BlockSpec((tm, WIDTH), lambda i, j: (i, j)),
        pl.BlockSpec((tm, WIDTH), lambda i, j: (i, 0)),
        pl.BlockSpec((tm, WIDTH), lambda i, j: (i, 0)),
        pl.BlockSpec((tm, WIDTH), lambda i, j: (i, 0)),
    ]
    if emit_transposed:
        nb = tm // MOBA_BLOCK
        out_shape.append(jax.ShapeDtypeStruct((rows // MOBA_BLOCK, WIDTH, MOBA_BLOCK), BF16))
        out_specs.append(pl.BlockSpec((nb, WIDTH, MOBA_BLOCK), lambda i, j: (i, 0, 0)))
        out_shape.append(jax.ShapeDtypeStruct(
            (rows // MOBA_BLOCK, N_HEADS, V_AUG_ROWS, MOBA_BLOCK), BF16))
        out_specs.append(pl.BlockSpec((nb, N_HEADS, V_AUG_ROWS, MOBA_BLOCK),
                                      lambda i, j: (i, 0, 0, 0)))
    return pl.pallas_call(
        functools.partial(_inproj_kernel, emit_transposed=emit_transposed),
        grid=(rows // tm, N_CHUNKS),
        in_specs=[
            pl.BlockSpec((tm, D_MODEL), lambda i, j: (i, 0)),
            _mod_spec(mod, tm, 1, 0),
            _mod_spec(mod, tm, 1, 1),
            pl.BlockSpec((D_MODEL, WIDTH), lambda i, j: (0, j)),
        ],
        out_specs=out_specs,
        out_shape=out_shape,
        scratch_shapes=[pltpu.VMEM((tm, D_MODEL), BF16)],
        compiler_params=_params("parallel", "arbitrary"),
        name="inproj",
    )(x, mod, mod, w_in)


def _decay_tables(lg, c, n):
    row = lax.broadcasted_iota(jnp.int32, (n, n), 0)
    col = lax.broadcasted_iota(jnp.int32, (n, n), 1)
    diff = (row - col).astype(F32)
    dmat = jnp.where(diff >= 0, jnp.exp(lg * jnp.maximum(diff, 0.0)), 0.0) * QK_SCALE
    i = lax.broadcasted_iota(jnp.int32, (n, HEAD_DIM), 0).astype(F32)
    cross = jnp.exp(lg * (i + 1.0))
    wtail = jnp.exp(lg * (c - 1.0 - i)) * QK_SCALE
    return dmat, cross, wtail


def _groupnorm_gate(o, g):
    mu = jnp.mean(o, axis=-1, keepdims=True)
    oc = o - mu
    var = jnp.mean(oc * oc, axis=-1, keepdims=True)
    return _silu(g) * (oc * lax.rsqrt(var + LN_EPS))


def _retention_chunk(q, k, kf, v, state, dmat, cross, wtail, gc):
    p = (_dot_nt(q, k) * dmat).astype(BF16)
    o = jnp.dot(p, v, preferred_element_type=F32)
    o = o + jnp.dot(q, state.astype(BF16), preferred_element_type=F32) * cross
    kw = (kf * wtail).astype(BF16)
    return o, gc * state + _dot_tn(kw, v)


def _ret_prompt_kernel(lg_ref, q_ref, k_ref, v_ref, g_ref, o_ref, state_ref,
                       dmat_ref, cross_ref, wtail_ref, *, chunk):
    c = pl.program_id(0)

    @pl.when(c == 0)
    def _():
        for h in range(N_HEADS):
            dmat_ref[h], cross_ref[h], wtail_ref[h] = _decay_tables(lg_ref[h], chunk, chunk)
        state_ref[...] = jnp.zeros_like(state_ref)

    for h in range(N_HEADS):
        cols = slice(h * HEAD_DIM, (h + 1) * HEAD_DIM)
        k = k_ref[:, cols]
        gc = jnp.exp(jnp.full((HEAD_DIM, HEAD_DIM), lg_ref[h] * chunk, F32))
        o, state = _retention_chunk(q_ref[:, cols], k, k.astype(F32), v_ref[:, cols], state_ref[h],
                                    dmat_ref[h], cross_ref[h], wtail_ref[h], gc)
        state_ref[h] = state
        o_ref[:, cols] = _groupnorm_gate(o, g_ref[:, cols]).astype(o_ref.dtype)


def _ret_prompt(zb, g, log_gamma, chunk):
    s = zb.shape[0]
    return pl.pallas_call(
        functools.partial(_ret_prompt_kernel, chunk=chunk),
        grid=(s // chunk,),
        in_specs=[
            pl.BlockSpec(memory_space=pltpu.SMEM),
            pl.BlockSpec((chunk, WIDTH), lambda c: (c, CH_QR)),
            pl.BlockSpec((chunk, WIDTH), lambda c: (c, CH_KR)),
            pl.BlockSpec((chunk, WIDTH), lambda c: (c, CH_VR)),
            pl.BlockSpec((chunk, WIDTH), lambda c: (c, 0)),
        ],
        out_specs=[
            pl.BlockSpec((chunk, WIDTH), lambda c: (c, 0)),
            pl.BlockSpec((N_HEADS, HEAD_DIM, HEAD_DIM), lambda c: (0, 0, 0)),
        ],
        out_shape=[
            jax.ShapeDtypeStruct((s, WIDTH), BF16),
            jax.ShapeDtypeStruct((N_HEADS, HEAD_DIM, HEAD_DIM), F32),
        ],
        scratch_shapes=[
            pltpu.VMEM((N_HEADS, chunk, chunk), F32),
            pltpu.VMEM((N_HEADS, chunk, HEAD_DIM), F32),
            pltpu.VMEM((N_HEADS, chunk, HEAD_DIM), F32),
        ],
        compiler_params=_params("arbitrary"),
        name="ret_prompt",
    )(log_gamma, zb, zb, zb, g)


def _ret_sample_kernel(lg_ref, z_ref, g_ref, st_ref, o_ref, sn_ref, *, t):
    n = z_ref.shape[0]
    for h in range(N_HEADS):
        lg = lg_ref[h]
        dmat, cross, wtail = _decay_tables(lg, t, n)

        def head(ch, h=h):
            c0 = (ch * N_HEADS + h) * HEAD_DIM
            return z_ref[:, c0:c0 + HEAD_DIM]

        kf = head(CH_KR)
        gc = jnp.exp(jnp.full((HEAD_DIM, HEAD_DIM), lg * t, F32))
        o, state = _retention_chunk(head(CH_QR).astype(BF16), kf.astype(BF16), kf,
                                    head(CH_VR).astype(BF16), st_ref[h], dmat, cross, wtail, gc)
        sn_ref[h] = state
        gh = g_ref[:, h * HEAD_DIM:(h + 1) * HEAD_DIM]
        o_ref[:, h * HEAD_DIM:(h + 1) * HEAD_DIM] = _groupnorm_gate(o, gh)


def _ret_sample(z3, g3, state, log_gamma, t):
    b, n, _ = z3.shape
    return pl.pallas_call(
        functools.partial(_ret_sample_kernel, t=t),
        grid=(b,),
        in_specs=[
            pl.BlockSpec(memory_space=pltpu.SMEM),
            pl.BlockSpec((None, n, IN_WIDTH), lambda i: (i, 0, 0)),
            pl.BlockSpec((None, n, WIDTH), lambda i: (i, 0, 0)),
            pl.BlockSpec((None, N_HEADS, HEAD_DIM, HEAD_DIM), lambda i: (i, 0, 0, 0)),
        ],
        out_specs=[
            pl.BlockSpec((None, n, WIDTH), lambda i: (i, 0, 0)),
            pl.BlockSpec((None, N_HEADS, HEAD_DIM, HEAD_DIM), lambda i: (i, 0, 0, 0)),
        ],
        out_shape=[
            jax.ShapeDtypeStruct((b, n, WIDTH), F32),
            jax.ShapeDtypeStruct(state.shape, F32),
        ],
        compiler_params=_params("parallel"),
        name="ret_sample",
    )(log_gamma, z3, g3, state)


def _kmean_kernel(k_ref, o_ref):
    nb = o_ref.shape[0]
    k = k_ref[...].reshape(nb, MOBA_BLOCK, WIDTH)
    o_ref[...] = jnp.sum(k, axis=1) * (1.0 / MOBA_BLOCK)


def _kmean_prompt(k):
    s = k.shape[0]
    nb = s // MOBA_BLOCK
    per = SUBLANES if nb % SUBLANES == 0 else nb
    return pl.pallas_call(
        _kmean_kernel,
        grid=(nb // per,),
        in_specs=[pl.BlockSpec((per * MOBA_BLOCK, WIDTH), lambda i: (i, 0))],
        out_specs=pl.BlockSpec((per, WIDTH), lambda i: (i, 0)),
        out_shape=jax.ShapeDtypeStruct((nb, WIDTH), F32),
        compiler_params=_params("parallel"),
        name="kmean_prompt",
    )(k)


def _top_mask_t(gate_t, n_valid, top):
    n = gate_t.shape[0]
    blk = lax.broadcasted_iota(jnp.int32, gate_t.shape, 0)
    g = jnp.where(blk < n_valid, gate_t, NEG_INF)
    sel = jnp.zeros(gate_t.shape, F32)
    for _ in range(top):
        m = jnp.max(g, axis=0, keepdims=True)
        first = jnp.min(jnp.where(g == m, blk, n), axis=0, keepdims=True)
        hit = blk == first
        finite = jnp.abs(m) < float("inf")
        sel = jnp.where(hit, jnp.where(finite, 1.0, sel), sel)
        g = jnp.where(hit, NEG_INF, g)
    return sel


def _moba_prompt_kernel(slope_ref, qt_ref, k_ref, vt_ref, km_ref, o_ref, sel_ref, kaug_ref,
                        *bufs):
    u_refs, acc_refs = bufs[:MOBA_CHAINS], bufs[MOBA_CHAINS:]
    h = pl.program_id(0)
    i = pl.program_id(1)
    slope = slope_ref[h]
    blk = MOBA_BLOCK
    nb = sel_ref.shape[0]

    @pl.when(i == 0)
    def _():
        kc = lax.broadcasted_iota(jnp.int32, (blk, HEAD_DIM), 0)
        col = lax.broadcasted_iota(jnp.int32, (blk, HEAD_DIM), 1)
        kaug_ref[...] = jnp.where(col == 0, slope * kc.astype(F32), 0.0).astype(BF16)

    qt = qt_ref[...]
    gate_t = jnp.dot(km_ref[...].astype(BF16), qt, preferred_element_type=F32)
    sel_ref[...] = _top_mask_t(gate_t, i, MOBA_TOPK)

    one_row = lax.broadcasted_iota(jnp.int32, (HEAD_DIM, blk), 0) == 0
    qt_aug = jnp.concatenate([qt, jnp.where(one_row, 1.0, 0.0).astype(BF16)], axis=0)
    kaug = kaug_ref[...]

    def scores_t(j):
        js = pl.multiple_of(j * blk, blk)
        kj = jnp.concatenate([k_ref[pl.ds(js, blk), :], kaug], axis=1)
        return jnp.dot(kj, qt_aug, preferred_element_type=F32)

    def values_t(j, p):
        r = jnp.dot(vt_ref[j], p.astype(BF16), preferred_element_type=F32)
        return r[HEAD_DIM:HEAD_DIM + 1], r[:HEAD_DIM]

    def block_of(r, c):
        return jnp.minimum(r * MOBA_CHAINS + c, nb - 1)

    half = MOBA_CHAINS // 2

    def issue_scores(r, phase):
        for c in range(half):
            chain = phase * half + c
            u_refs[chain][...] = scores_t(block_of(r, chain))

    issue_scores(0, 0)

    u0 = scores_t(i)
    kc = lax.broadcasted_iota(jnp.int32, (blk, blk), 0)
    qr = lax.broadcasted_iota(jnp.int32, (blk, blk), 1)
    u0 = jnp.where(kc <= qr, u0, NEG_INF)
    m0 = jnp.max(u0, axis=0, keepdims=True)
    l0, acc0 = values_t(i, jnp.exp(u0 - m0))
    acc_refs[0][...] = acc0
    for c in range(1, MOBA_CHAINS):
        acc_refs[c][...] = jnp.zeros((HEAD_DIM, blk), F32)

    def chain_step(r, phase, c, m, l):
        chain = phase * half + c
        j_raw = r * MOBA_CHAINS + chain
        j = block_of(r, chain)
        picked = jnp.where(j_raw < i, sel_ref[pl.ds(j, 1), :], 0.0) > 0.0
        off = -slope * ((i - j) * blk).astype(F32)
        u_ref, acc_ref = u_refs[chain], acc_refs[chain]
        mj = jnp.max(u_ref[...], axis=0, keepdims=True) + off
        m_new = jnp.where(picked, jnp.maximum(m, mj), m)
        p = jnp.exp(u_ref[...] - jnp.where(picked, m_new - off, float("inf")))
        alpha = jnp.where(m_new == NEG_INF, 1.0, jnp.exp(m - m_new))
        lj, accj = values_t(j, p)
        acc_ref[...] = alpha * acc_ref[...] + accj
        return m_new, alpha * l + lj

    def body(r, carry):
        ms, ls = list(carry[0]), list(carry[1])
        for phase in range(2):
            issue_scores(r + phase, 1 - phase)
            for c in range(half):
                chain = phase * half + c
                ms[chain], ls[chain] = chain_step(r, phase, c, ms[chain], ls[chain])
        return tuple(ms), tuple(ls)

    ms0 = (m0,) + (jnp.full((1, blk), NEG_INF, F32),) * (MOBA_CHAINS - 1)
    ls0 = (l0,) + (jnp.zeros((1, blk), F32),) * (MOBA_CHAINS - 1)
    ms, ls = lax.fori_loop(0, lax.div(i + MOBA_CHAINS - 1, MOBA_CHAINS), body, (ms0, ls0))

    m_all = functools.reduce(jnp.maximum, ms)
    l = jnp.zeros((1, blk), F32)
    acc = jnp.zeros((HEAD_DIM, blk), F32)
    for c in range(MOBA_CHAINS):
        w = jnp.where(ms[c] == NEG_INF, 0.0, jnp.exp(ms[c] - m_all))
        l = l + w * ls[c]
        acc = acc + w * acc_refs[c][...]
    o_ref[...] = (acc / l).T.astype(o_ref.dtype)


def _moba_prompt(zb, qt, vt, kmean, slopes):
    s = zb.shape[0]
    nb = s // MOBA_BLOCK
    return pl.pallas_call(
        _moba_prompt_kernel,
        grid=(N_HEADS, nb),
        in_specs=[
            pl.BlockSpec(memory_space=pltpu.SMEM),
            pl.BlockSpec((None, HEAD_DIM, MOBA_BLOCK), lambda h, i: (i, h, 0)),
            pl.BlockSpec((s, HEAD_DIM), lambda h, i: (0, CH_KA * N_HEADS + h)),
            pl.BlockSpec((nb, None, V_AUG_ROWS, MOBA_BLOCK), lambda h, i: (0, h, 0, 0)),
            pl.BlockSpec((nb, HEAD_DIM), lambda h, i: (0, h)),
        ],
        out_specs=pl.BlockSpec((MOBA_BLOCK, HEAD_DIM), lambda h, i: (i, h)),
        out_shape=jax.ShapeDtypeStruct((s, WIDTH), BF16),
        scratch_shapes=[
            pltpu.VMEM((nb, MOBA_BLOCK), F32),
            pltpu.VMEM((MOBA_BLOCK, HEAD_DIM), BF16),
        ] + [pltpu.VMEM((MOBA_BLOCK, MOBA_BLOCK), F32)] * MOBA_CHAINS
          + [pltpu.VMEM((HEAD_DIM, MOBA_BLOCK), F32)] * MOBA_CHAINS,
        compiler_params=_params("parallel", "arbitrary"),
        name="moba_prompt",
    )(slopes, qt, zb, vt, kmean)


PAGES_PER_STEP = SUBLANES * PAGES_PER_BLOCK


def _kmean_sample_kernel(pt_ref, *refs):
    del pt_ref
    page_refs, o_ref = refs[:-1], refs[-1]
    for n in range(len(page_refs) // PAGES_PER_BLOCK):
        tot = None
        for r in range(PAGES_PER_BLOCK):
            part = jnp.sum(page_refs[n * PAGES_PER_BLOCK + r][...], axis=0)
            tot = part if tot is None else tot + part
        o_ref[n] = tot * (1.0 / MOBA_BLOCK)


def _kmean_sample(cache_k, page_table):
    b, n_pages = page_table.shape
    n_full = n_pages // PAGES_PER_BLOCK
    pps = PAGES_PER_STEP if n_pages % PAGES_PER_STEP == 0 else n_pages
    bps = pps // PAGES_PER_BLOCK

    def page_spec(r):
        return pl.BlockSpec((None, PAGE_SIZE, N_HEADS, HEAD_DIM),
                            lambda i, s, pt: (pt[i, s * pps + r], 0, 0, 0))

    grid_spec = pltpu.PrefetchScalarGridSpec(
        num_scalar_prefetch=1,
        grid=(b, n_pages // pps),
        in_specs=[page_spec(r) for r in range(pps)],
        out_specs=pl.BlockSpec((None, bps, N_HEADS, HEAD_DIM), lambda i, s, pt: (i, s, 0, 0)),
    )
    return pl.pallas_call(
        _kmean_sample_kernel,
        grid_spec=grid_spec,
        out_shape=jax.ShapeDtypeStruct((b, n_full, N_HEADS, HEAD_DIM), F32),
        compiler_params=_params("parallel", "arbitrary"),
        name="kmean_sample",
    )(page_table, *([cache_k] * pps))


def _gate_sample_kernel(z_ref, km_ref, o_ref):
    rows = z_ref.shape[0]
    n = km_ref.shape[0]
    lane = lax.broadcasted_iota(jnp.int32, (rows, LANES), 1)
    blk = lax.broadcasted_iota(jnp.int32, (rows, n), 1)
    out = jnp.zeros((rows, LANES), jnp.int32)
    for h in range(N_HEADS):
        c0 = (CH_QA * N_HEADS + h) * HEAD_DIM
        q = z_ref[:, c0:c0 + HEAD_DIM].astype(BF16)
        kmh = km_ref[:, h, :].astype(BF16)
        g = _dot_nt(q, kmh)
        for r in range(MOBA_TOPK):
            m = jnp.max(g, axis=1, keepdims=True)
            first = jnp.min(jnp.where(g == m, blk, n), axis=1, keepdims=True)
            out = jnp.where(lane == h * MOBA_TOPK + r, first, out)
            g = jnp.where(blk == first, NEG_INF, g)
    o_ref[...] = out


def _gate_sample(z3, kmean_s):
    b, rows, _ = z3.shape
    n = kmean_s.shape[1]
    return pl.pallas_call(
        _gate_sample_kernel,
        grid=(b,),
        in_specs=[
            pl.BlockSpec((None, rows, IN_WIDTH), lambda i: (i, 0, 0)),
            pl.BlockSpec((None, n, N_HEADS, HEAD_DIM), lambda i: (i, 0, 0, 0)),
        ],
        out_specs=pl.BlockSpec((None, rows, LANES), lambda i: (i, 0, 0)),
        out_shape=jax.ShapeDtypeStruct((b, rows, LANES), jnp.int32),
        compiler_params=_params("parallel"),
        name="gate_sample",
    )(z3, kmean_s)


def _attn_sample_kernel(pt_ref, idx_ref, slope_ref, q_ref, kn_ref, vn_ref, ck_hbm, cv_hbm, o_ref,
                        kbuf, vbuf, sem, *, t, past):
    per_tok = MOBA_TOPK * PAGES_PER_BLOCK
    b = pl.program_id(0)
    h = pl.program_id(1)
    step = b * N_HEADS + h
    slot = lax.rem(step, 2)

    def copies(bb, hh, sl):
        out = []
        for tt in range(t):
            for r in range(MOBA_TOPK):
                blk = idx_ref[((bb * t + tt) * N_HEADS + hh) * MOBA_TOPK + r]
                for pg in range(PAGES_PER_BLOCK):
                    phys = pt_ref[bb, blk * PAGES_PER_BLOCK + pg]
                    j = tt * per_tok + r * PAGES_PER_BLOCK + pg
                    out.append(pltpu.make_async_copy(ck_hbm.at[phys, :, hh, :], kbuf.at[sl, j],
                                                     sem.at[sl]))
                    out.append(pltpu.make_async_copy(cv_hbm.at[phys, :, hh, :], vbuf.at[sl, j],
                                                     sem.at[sl]))
        return out

    @pl.when(step == 0)
    def _():
        for cp in copies(b, h, slot):
            cp.start()

    @pl.when(step + 1 < pl.num_programs(0) * N_HEADS)
    def _():
        last_head = h == N_HEADS - 1
        for cp in copies(jnp.where(last_head, b + 1, b), jnp.where(last_head, 0, h + 1), 1 - slot):
            cp.start()

    for cp in copies(b, h, slot):
        cp.wait()

    k_refs = [kbuf.at[slot, j] for j in range(t * per_tok)]
    v_refs = [vbuf.at[slot, j] for j in range(t * per_tok)]
    slope = slope_ref[h]
    rows = q_ref.shape[0]
    q = q_ref[...].astype(BF16)
    zpad = jnp.zeros((LANES - rows, HEAD_DIM), F32)
    kn = jnp.concatenate([kn_ref[...], zpad], axis=0)
    vn = jnp.concatenate([vn_ref[...], zpad], axis=0)
    lane_blk = lax.broadcasted_iota(jnp.int32, (1, MOBA_BLOCK), 1)
    lane_new = lax.broadcasted_iota(jnp.int32, (1, LANES), 1)
    for tt in range(t):
        ks = jnp.concatenate([k_refs[tt * per_tok + r][...] for r in range(per_tok)] + [kn], axis=0)
        vs = jnp.concatenate([v_refs[tt * per_tok + r][...] for r in range(per_tok)] + [vn], axis=0)
        s = _dot_nt(q, ks.astype(BF16))[tt:tt + 1]
        dist = []
        for r in range(MOBA_TOPK):
            blk = idx_ref[((b * t + tt) * N_HEADS + h) * MOBA_TOPK + r]
            dist.append((past + tt - blk * MOBA_BLOCK - lane_blk).astype(F32))
        dist.append((tt - lane_new).astype(F32))
        valid = jnp.concatenate([jnp.ones((1, MOBA_TOPK * MOBA_BLOCK), jnp.int32),
                                 (lane_new <= tt).astype(jnp.int32)], axis=1)
        s = jnp.where(valid > 0, s - slope * jnp.concatenate(dist, axis=1), NEG_INF)
        m = jnp.max(s, axis=1, keepdims=True)
        p = jnp.exp(s - m)
        l = jnp.sum(p, axis=1, keepdims=True)
        p8 = jnp.broadcast_to(p, (SUBLANES, p.shape[1])).astype(BF16)
        o = jnp.dot(p8, vs.astype(BF16), preferred_element_type=F32)[0:1]
        o_ref[tt:tt + 1, :] = o / l
    if rows > t:
        o_ref[t:, :] = jnp.zeros((rows - t, HEAD_DIM), F32)


def _attn_sample(z3, cache_k, cache_v, page_table, idx_flat, slopes, t, past):
    b, rows, _ = z3.shape
    n_sel = t * MOBA_TOPK * PAGES_PER_BLOCK

    def tok_spec(ch):
        return pl.BlockSpec((None, rows, HEAD_DIM), lambda i, h, pt, idx: (i, 0, ch * N_HEADS + h))

    grid_spec = pltpu.PrefetchScalarGridSpec(
        num_scalar_prefetch=2,
        grid=(b, N_HEADS),
        in_specs=[pl.BlockSpec(memory_space=pltpu.SMEM), tok_spec(CH_QA), tok_spec(CH_KA),
                  tok_spec(CH_VA), pl.BlockSpec(memory_space=pl.ANY),
                  pl.BlockSpec(memory_space=pl.ANY)],
        out_specs=pl.BlockSpec((None, rows, HEAD_DIM), lambda i, h, pt, idx: (i, 0, h)),
        scratch_shapes=[
            pltpu.VMEM((2, n_sel, PAGE_SIZE, HEAD_DIM), F32),
            pltpu.VMEM((2, n_sel, PAGE_SIZE, HEAD_DIM), F32),
            pltpu.SemaphoreType.DMA((2,)),
        ],
    )
    return pl.pallas_call(
        functools.partial(_attn_sample_kernel, t=t, past=past),
        grid_spec=grid_spec,
        out_shape=jax.ShapeDtypeStruct((b, rows, WIDTH), F32),
        compiler_params=_params("arbitrary", "arbitrary"),
        name="attn_sample",
    )(page_table, idx_flat, slopes, z3, z3, z3, cache_k, cache_v)


def _outproj_kernel(x_ref, gate_ref, ret_ref, att_ref, wr_ref, wa_ref, g_ref, b_ref, o_ref):
    y = jnp.dot(ret_ref[...].astype(BF16), wr_ref[...], preferred_element_type=F32)
    y = y + jnp.dot(att_ref[...].astype(BF16), wa_ref[...], preferred_element_type=F32)
    r = ALPHA * x_ref[...] + gate_ref[...] * y
    o_ref[...] = _layernorm(r, g_ref[...], b_ref[...])


def _outproj(x, mod, ret, att, w_o, ln_g, ln_b, tm):
    rows = x.shape[0]
    return pl.pallas_call(
        _outproj_kernel,
        grid=(rows // tm,),
        in_specs=[
            pl.BlockSpec((tm, D_MODEL), lambda i: (i, 0)),
            _mod_spec(mod, tm, 1, 2),
            pl.BlockSpec((tm, WIDTH), lambda i: (i, 0)),
            pl.BlockSpec((tm, WIDTH), lambda i: (i, 0)),
            pl.BlockSpec((WIDTH, D_MODEL), lambda i: (0, 0)),
            pl.BlockSpec((WIDTH, D_MODEL), lambda i: (1, 0)),
            pl.BlockSpec((1, D_MODEL), lambda i: (0, 0)),
            pl.BlockSpec((1, D_MODEL), lambda i: (0, 0)),
        ],
        out_specs=pl.BlockSpec((tm, D_MODEL), lambda i: (i, 0)),
        out_shape=jax.ShapeDtypeStruct((rows, D_MODEL), F32),
        compiler_params=_params("parallel"),
        name="outproj",
    )(x, mod, ret, att, w_o, w_o, ln_g[1:2], ln_b[1:2])


def _row_tile(rows, want):
    return want if rows % want == 0 else rows


def kernel(x_prompt, x_sample, cache_k, cache_v, state_ret, page_table, c_prompt, c_sample,
           w_ada, b_ada, w_ffn1_in, w_ffn1_out, w_ffn2_in, w_ffn2_out, w_in, w_o, ln_g, ln_b):
    assert w_ada.shape[0] == DEPTH and x_prompt.shape[0] == 1
    seq = x_prompt.shape[1]
    bd, t, _ = x_sample.shape
    n_pages = page_table.shape[1]
    past = n_pages * PAGE_SIZE
    assert seq % MOBA_BLOCK == 0 and n_pages % PAGES_PER_BLOCK == 0
    assert n_pages // PAGES_PER_BLOCK >= MOBA_TOPK and t <= SUBLANES

    hs = jnp.arange(N_HEADS, dtype=F32)
    log_gamma = jnp.log1p(-jnp.exp2(-5.0 - hs))
    slopes = jnp.exp2(-8.0 * (hs + 1.0) / N_HEADS)

    w1i, w1o = w_ffn1_in[0].astype(BF16), w_ffn1_out[0].astype(BF16)
    w2i, w2o = w_ffn2_in[0].astype(BF16), w_ffn2_out[0].astype(BF16)
    wi, wo = w_in[0].astype(BF16), w_o[0].astype(BF16)
    g_ln, b_ln = ln_g[0], ln_b[0]

    pad = (-(1 + bd)) % SUBLANES
    c_all = jnp.concatenate([c_prompt, c_sample, jnp.zeros((pad, D_MODEL), F32)], axis=0)
    mod = _adaln(c_all, w_ada[0], b_ada[0])
    mod_p = mod[0:1]
    mod_s = jnp.repeat(mod[1:1 + bd], t, axis=0)

    xp = x_prompt[0]
    tm_p = _row_tile(seq, 512)
    tf = 512
    xp = _ffn(xp, mod_p, 0, w1i, w1o, g_ln, b_ln, tm_p, tf)
    zb, g_r, k_a, v_a, qt, vt = _inproj(xp, mod_p, wi, tm_p, BF16, True)
    ret, state_p = _ret_prompt(zb, g_r, log_gamma, MOBA_BLOCK)
    kmean_p = _kmean_prompt(k_a)
    att = _moba_prompt(zb, qt, vt, kmean_p, slopes)
    xp = _outproj(xp, mod_p, ret, att, wo, g_ln, b_ln, tm_p)
    xp = _ffn(xp, mod_p, 2, w2i, w2o, g_ln, b_ln, tm_p, tf)

    rows_s = bd * t
    xs = x_sample.reshape(rows_s, D_MODEL)
    xs = _ffn(xs, mod_s, 0, w1i, w1o, g_ln, b_ln, rows_s, tf)
    zs, gs, ks, vs = _inproj(xs, mod_s, wi, rows_s, F32, False)
    tpad = ((0, 0), (0, SUBLANES - t), (0, 0))
    zs3 = jnp.pad(zs.reshape(bd, t, IN_WIDTH), tpad)
    gs3 = jnp.pad(gs.reshape(bd, t, WIDTH), tpad)
    ret_s, state_s = _ret_sample(zs3, gs3, state_ret[0], log_gamma, t)
    kmean_s = _kmean_sample(cache_k[0], page_table)
    idx = _gate_sample(zs3, kmean_s)[:, :t, :N_HEADS * MOBA_TOPK].reshape(-1)
    att_s = _attn_sample(zs3, cache_k[0], cache_v[0], page_table, idx, slopes, t, past)
    xs = _outproj(xs, mod_s, ret_s[:, :t].reshape(rows_s, WIDTH),
                  att_s[:, :t].reshape(rows_s, WIDTH), wo, g_ln, b_ln, rows_s)
    xs = _ffn(xs, mod_s, 2, w2i, w2o, g_ln, b_ln, rows_s, tf)

    hd = (N_HEADS, HEAD_DIM)
    return (xp[None], xs.reshape(bd, t, D_MODEL),
            k_a.reshape(1, 1, seq, *hd), v_a.reshape(1, 1, seq, *hd), state_p[None, None],
            ks.reshape(1, bd, t, *hd), vs.reshape(1, bd, t, *hd), state_s[None])
```

```python
import functools

import jax
import jax.numpy as jnp
from jax import lax
from jax.experimental import pallas as pl
from jax.experimental.pallas import tpu as pltpu

F32 = jnp.float32
BF16 = jnp.bfloat16

D_MODEL = 2048
N_HEADS = 8
HEAD_DIM = 128
WIDTH = N_HEADS * HEAD_DIM
N_CHUNKS = 7
IN_WIDTH = N_CHUNKS * WIDTH
D_FF = 5632
MOBA_BLOCK = 256
MOBA_TOPK = 3
PAGE_SIZE = 128
PAGES_PER_BLOCK = MOBA_BLOCK // PAGE_SIZE
LN_EPS = 1e-5
DEPTH = 1
ALPHA = (2 * DEPTH) ** 0.25
FFN_RESIDUAL = 0.5
QK_SCALE = HEAD_DIM ** -0.5
NEG_INF = float("-inf")
SUBLANES = 8
BF16_SUBLANES = 16
LANES = 128
V_AUG_ROWS = HEAD_DIM + BF16_SUBLANES
MOBA_CHAINS = 4

VMEM_LIMIT_BYTES = 56 * 1024 * 1024

CH_QR, CH_KR, CH_VR, CH_GR, CH_QA, CH_KA, CH_VA = range(N_CHUNKS)


def _params(*sem):
    return pltpu.CompilerParams(dimension_semantics=sem, vmem_limit_bytes=VMEM_LIMIT_BYTES)


def _silu(x):
    return x * jax.nn.sigmoid(x)


def _layernorm(r, g, b):
    mu = jnp.mean(r, axis=-1, keepdims=True)
    xc = r - mu
    var = jnp.mean(xc * xc, axis=-1, keepdims=True)
    return xc * lax.rsqrt(var + LN_EPS) * g + b


def _dot_nt(a, b):
    return lax.dot_general(a, b, (((1,), (1,)), ((), ())), preferred_element_type=F32)


def _dot_tn(a, b):
    return lax.dot_general(a, b, (((0,), (0,)), ((), ())), preferred_element_type=F32)


def _adaln_kernel(c_ref, w_ref, b_ref, o_ref):
    a = _silu(c_ref[...]).astype(BF16)
    o_ref[...] = jnp.dot(a, w_ref[...].astype(BF16), preferred_element_type=F32) + b_ref[...]


def _adaln(c_all, w_ada, b_ada):
    rows = c_all.shape[0]
    n = w_ada.shape[1]
    tn = 1024
    return pl.pallas_call(
        _adaln_kernel,
        grid=(n // tn,),
        in_specs=[
            pl.BlockSpec((rows, D_MODEL), lambda j: (0, 0)),
            pl.BlockSpec((D_MODEL, tn), lambda j: (0, j)),
            pl.BlockSpec((1, tn), lambda j: (0, j)),
        ],
        out_specs=pl.BlockSpec((rows, tn), lambda j: (0, j)),
        out_shape=jax.ShapeDtypeStruct((rows, n), F32),
        compiler_params=_params("arbitrary"),
        name="adaln",
    )(c_all, w_ada, b_ada.reshape(1, n))


def _mod_spec(mod, tm, sub, part):
    col = sub * 3 + part
    if mod.shape[0] == 1:
        return pl.BlockSpec((1, D_MODEL), lambda i, *_: (0, col))
    return pl.BlockSpec((tm, D_MODEL), lambda i, *_: (i, col))


def _ffn_kernel(x_ref, shift_ref, scale_ref, gate_ref, wa_ref, wu_ref, wo_ref, g_ref, b_ref,
                o_ref, xm_ref, acc_ref):
    f = pl.program_id(1)

    @pl.when(f == 0)
    def _():
        xm_ref[...] = (x_ref[...] * (1.0 + scale_ref[...]) + shift_ref[...]).astype(BF16)
        acc_ref[...] = jnp.zeros_like(acc_ref)

    xm = xm_ref[...]
    a = jnp.dot(xm, wa_ref[...], preferred_element_type=F32)
    u = jnp.dot(xm, wu_ref[...], preferred_element_type=F32)
    h = (_silu(a) * u).astype(BF16)
    acc_ref[...] += jnp.dot(h, wo_ref[...], preferred_element_type=F32)

    @pl.when(f == pl.num_programs(1) - 1)
    def _():
        r = ALPHA * x_ref[...] + FFN_RESIDUAL * gate_ref[...] * acc_ref[...]
        o_ref[...] = _layernorm(r, g_ref[...], b_ref[...])


def _ffn(x, mod, sub, w_in, w_out, ln_g, ln_b, tm, tf):
    rows = x.shape[0]
    nf = D_FF // tf
    return pl.pallas_call(
        _ffn_kernel,
        grid=(rows // tm, nf),
        in_specs=[
            pl.BlockSpec((tm, D_MODEL), lambda i, f: (i, 0)),
            _mod_spec(mod, tm, sub, 0),
            _mod_spec(mod, tm, sub, 1),
            _mod_spec(mod, tm, sub, 2),
            pl.BlockSpec((D_MODEL, tf), lambda i, f: (0, f)),
            pl.BlockSpec((D_MODEL, tf), lambda i, f: (0, f + nf)),
            pl.BlockSpec((tf, D_MODEL), lambda i, f: (f, 0)),
            pl.BlockSpec((1, D_MODEL), lambda i, f: (0, 0)),
            pl.BlockSpec((1, D_MODEL), lambda i, f: (0, 0)),
        ],
        out_specs=pl.BlockSpec((tm, D_MODEL), lambda i, f: (i, 0)),
        out_shape=jax.ShapeDtypeStruct((rows, D_MODEL), F32),
        scratch_shapes=[pltpu.VMEM((tm, D_MODEL), BF16), pltpu.VMEM((tm, D_MODEL), F32)],
        compiler_params=_params("parallel", "arbitrary"),
        name="ffn",
    )(x, mod, mod, mod, w_in, w_in, w_out, ln_g[sub:sub + 1], ln_b[sub:sub + 1])


def _inproj_kernel(x_ref, shift_ref, scale_ref, w_ref, z_ref, g_ref, k_ref, v_ref, *rest,
                   emit_transposed):
    if emit_transposed:
        qt_ref, vt_ref, xm_ref = rest
    else:
        (xm_ref,) = rest
    j = pl.program_id(1)

    @pl.when(j == 0)
    def _():
        xm_ref[...] = (x_ref[...] * (1.0 + scale_ref[...]) + shift_ref[...]).astype(BF16)

    z = jnp.dot(xm_ref[...], w_ref[...], preferred_element_type=F32)

    def put_transposed(ref, val):
        zt = val.T.astype(BF16)
        for n in range(ref.shape[0]):
            blk = zt[:, n * MOBA_BLOCK:(n + 1) * MOBA_BLOCK]
            if ref.ndim == 3:
                ref[n] = blk
            else:
                for h in range(N_HEADS):
                    ref[n, h, :HEAD_DIM, :] = blk[h * HEAD_DIM:(h + 1) * HEAD_DIM]
                    ref[n, h, HEAD_DIM:, :] = jnp.ones((V_AUG_ROWS - HEAD_DIM, MOBA_BLOCK), BF16)

    @pl.when(j != CH_QA)
    def _():
        z_ref[...] = z.astype(z_ref.dtype)

    @pl.when(j == CH_QA)
    def _():
        zs = z * QK_SCALE
        z_ref[...] = zs.astype(z_ref.dtype)
        if emit_transposed:
            put_transposed(qt_ref, zs)

    @pl.when(j == CH_GR)
    def _():
        g_ref[...] = z

    @pl.when(j == CH_KA)
    def _():
        k_ref[...] = z

    @pl.when(j == CH_VA)
    def _():
        v_ref[...] = z
        if emit_transposed:
            put_transposed(vt_ref, z)


def _inproj(x, mod, w_in, tm, z_dtype, emit_transposed):
    rows = x.shape[0]
    out_shape = [
        jax.ShapeDtypeStruct((rows, IN_WIDTH), z_dtype),
        jax.ShapeDtypeStruct((rows, WIDTH), F32),
        jax.ShapeDtypeStruct((rows, WIDTH), F32),
        jax.ShapeDtypeStruct((rows, WIDTH), F32),
    ]
    out_specs = [
        pl.BlockSpec((tm, WIDTH), lambda i, j: (i, j)),
        pl.BlockSpec((tm, WIDTH), lambda i, j: (i, 0)),
        pl.BlockSpec((tm, WIDTH), lambda i, j: (i, 0)),
        pl.BlockSpec((tm, WIDTH), lambda i, j: (i, 0)),
    ]
    if emit_transposed:
        nb = tm // MOBA_BLOCK
        out_shape.append(jax.ShapeDtypeStruct((rows // MOBA_BLOCK, WIDTH, MOBA_BLOCK), BF16))
        out_specs.append(pl.BlockSpec((nb, WIDTH, MOBA_BLOCK), lambda i, j: (i, 0, 0)))
        out_shape.append(jax.ShapeDtypeStruct(
            (rows // MOBA_BLOCK, N_HEADS, V_AUG_ROWS, MOBA_BLOCK), BF16))
        out_specs.append(pl.BlockSpec((nb, N_HEADS, V_AUG_ROWS, MOBA_BLOCK),
                                      lambda i, j: (i, 0, 0, 0)))
    return pl.pallas_call(
        functools.partial(_inproj_kernel, emit_transposed=emit_transposed),
        grid=(rows // tm, N_CHUNKS),
        in_specs=[
            pl.BlockSpec((tm, D_MODEL), lambda i, j: (i, 0)),
            _mod_spec(mod, tm, 1, 0),
            _mod_spec(mod, tm, 1, 1),
            pl.BlockSpec((D_MODEL, WIDTH), lambda i, j: (0, j)),
        ],
        out_specs=out_specs,
        out_shape=out_shape,
        scratch_shapes=[pltpu.VMEM((tm, D_MODEL), BF16)],
        compiler_params=_params("parallel", "arbitrary"),
        name="inproj",
    )(x, mod, mod, w_in)


def _decay_tables(lg, c, n):
    row = lax.broadcasted_iota(jnp.int32, (n, n), 0)
    col = lax.broadcasted_iota(jnp.int32, (n, n), 1)
    diff = (row - col).astype(F32)
    dmat = jnp.where(diff >= 0, jnp.exp(lg * jnp.maximum(diff, 0.0)), 0.0) * QK_SCALE
    i = lax.broadcasted_iota(jnp.int32, (n, HEAD_DIM), 0).astype(F32)
    cross = jnp.exp(lg * (i + 1.0))
    wtail = jnp.exp(lg * (c - 1.0 - i)) * QK_SCALE
    return dmat, cross, wtail


def _groupnorm_gate(o, g):
    mu = jnp.mean(o, axis=-1, keepdims=True)
    oc = o - mu
    var = jnp.mean(oc * oc, axis=-1, keepdims=True)
    return _silu(g) * (oc * lax.rsqrt(var + LN_EPS))


def _retention_heads(heads):
    first = []
    for q, k, kf, v, state, _, _, wtail, _ in heads:
        first.append((_dot_nt(q, k),
                      jnp.dot(q, state.astype(BF16), preferred_element_type=F32),
                      _dot_tn((kf * wtail).astype(BF16), v)))
    out = []
    for (_, _, _, v, state, dmat, cross, _, gc), (s, o_cross, kv) in zip(heads, first):
        o = jnp.dot((s * dmat).astype(BF16), v, preferred_element_type=F32) + o_cross * cross
        out.append((o, gc * state + kv))
    return out


def _ret_prompt_kernel(lg_ref, q_ref, k_ref, v_ref, g_ref, o_ref, state_ref,
                       dmat_ref, cross_ref, wtail_ref, *, chunk):
    c = pl.program_id(0)

    @pl.when(c == 0)
    def _():
        for h in range(N_HEADS):
            dmat_ref[h], cross_ref[h], wtail_ref[h] = _decay_tables(lg_ref[h], chunk, chunk)
        state_ref[...] = jnp.zeros_like(state_ref)

    heads = []
    for h in range(N_HEADS):
        cols = slice(h * HEAD_DIM, (h + 1) * HEAD_DIM)
        k = k_ref[:, cols]
        gc = jnp.exp(jnp.full((HEAD_DIM, HEAD_DIM), lg_ref[h] * chunk, F32))
        heads.append((q_ref[:, cols], k, k.astype(F32), v_ref[:, cols], state_ref[h],
                      dmat_ref[h], cross_ref[h], wtail_ref[h], gc))
    for h, (o, state) in enumerate(_retention_heads(heads)):
        cols = slice(h * HEAD_DIM, (h + 1) * HEAD_DIM)
        state_ref[h] = state
        o_ref[:, cols] = _groupnorm_gate(o, g_ref[:, cols]).astype(o_ref.dtype)


def _ret_prompt(zb, g, log_gamma, chunk):
    s = zb.shape[0]
    return pl.pallas_call(
        functools.partial(_ret_prompt_kernel, chunk=chunk),
        grid=(s // chunk,),
        in_specs=[
            pl.BlockSpec(memory_space=pltpu.SMEM),
            pl.BlockSpec((chunk, WIDTH), lambda c: (c, CH_QR)),
            pl.BlockSpec((chunk, WIDTH), lambda c: (c, CH_KR)),
            pl.BlockSpec((chunk, WIDTH), lambda c: (c, CH_VR)),
            pl.BlockSpec((chunk, WIDTH), lambda c: (c, 0)),
        ],
        out_specs=[
            pl.BlockSpec((chunk, WIDTH), lambda c: (c, 0)),
            pl.BlockSpec((N_HEADS, HEAD_DIM, HEAD_DIM), lambda c: (0, 0, 0)),
        ],
        out_shape=[
            jax.ShapeDtypeStruct((s, WIDTH), BF16),
            jax.ShapeDtypeStruct((N_HEADS, HEAD_DIM, HEAD_DIM), F32),
        ],
        scratch_shapes=[
            pltpu.VMEM((N_HEADS, chunk, chunk), F32),
            pltpu.VMEM((N_HEADS, chunk, HEAD_DIM), F32),
            pltpu.VMEM((N_HEADS, chunk, HEAD_DIM), F32),
        ],
        compiler_params=_params("arbitrary"),
        name="ret_prompt",
    )(log_gamma, zb, zb, zb, g)


def _ret_sample_kernel(lg_ref, z_ref, g_ref, st_ref, o_ref, sn_ref, *, t):
    n = z_ref.shape[0]
    heads = []
    for h in range(N_HEADS):
        lg = lg_ref[h]
        dmat, cross, wtail = _decay_tables(lg, t, n)

        def head(ch, h=h):
            c0 = (ch * N_HEADS + h) * HEAD_DIM
            return z_ref[:, c0:c0 + HEAD_DIM]

        kf = head(CH_KR)
        gc = jnp.exp(jnp.full((HEAD_DIM, HEAD_DIM), lg * t, F32))
        heads.append((head(CH_QR).astype(BF16), kf.astype(BF16), kf, head(CH_VR).astype(BF16),
                      st_ref[h], dmat, cross, wtail, gc))
    for h, (o, state) in enumerate(_retention_heads(heads)):
        sn_ref[h] = state
        gh = g_ref[:, h * HEAD_DIM:(h + 1) * HEAD_DIM]
        o_ref[:, h * HEAD_DIM:(h + 1) * HEAD_DIM] = _groupnorm_gate(o, gh)


def _ret_sample(z3, g3, state, log_gamma, t):
    b, n, _ = z3.shape
    return pl.pallas_call(
        functools.partial(_ret_sample_kernel, t=t),
        grid=(b,),
        in_specs=[
            pl.BlockSpec(memory_space=pltpu.SMEM),
            pl.BlockSpec((None, n, IN_WIDTH), lambda i: (i, 0, 0)),
            pl.BlockSpec((None, n, WIDTH), lambda i: (i, 0, 0)),
            pl.BlockSpec((None, N_HEADS, HEAD_DIM, HEAD_DIM), lambda i: (i, 0, 0, 0)),
        ],
        out_specs=[
            pl.BlockSpec((None, n, WIDTH), lambda i: (i, 0, 0)),
            pl.BlockSpec((None, N_HEADS, HEAD_DIM, HEAD_DIM), lambda i: (i, 0, 0, 0)),
        ],
        out_shape=[
            jax.ShapeDtypeStruct((b, n, WIDTH), F32),
            jax.ShapeDtypeStruct(state.shape, F32),
        ],
        compiler_params=_params("parallel"),
        name="ret_sample",
    )(log_gamma, z3, g3, state)


def _kmean_kernel(k_ref, o_ref):
    nb = o_ref.shape[0]
    k = k_ref[...].reshape(nb, MOBA_BLOCK, WIDTH)
    o_ref[...] = jnp.sum(k, axis=1) * (1.0 / MOBA_BLOCK)


def _kmean_prompt(k):
    s = k.shape[0]
    nb = s // MOBA_BLOCK
    per = SUBLANES if nb % SUBLANES == 0 else nb
    return pl.pallas_call(
        _kmean_kernel,
        grid=(nb // per,),
        in_specs=[pl.BlockSpec((per * MOBA_BLOCK, WIDTH), lambda i: (i, 0))],
        out_specs=pl.BlockSpec((per, WIDTH), lambda i: (i, 0)),
        out_shape=jax.ShapeDtypeStruct((nb, WIDTH), F32),
        compiler_params=_params("parallel"),
        name="kmean_prompt",
    )(k)


def _top_mask_t(gate_t, n_valid, top):
    n = gate_t.shape[0]
    blk = lax.broadcasted_iota(jnp.int32, gate_t.shape, 0)
    g = jnp.where(blk < n_valid, gate_t, NEG_INF)
    sel = jnp.zeros(gate_t.shape, F32)
    for _ in range(top):
        m = jnp.max(g, axis=0, keepdims=True)
        first = jnp.min(jnp.where(g == m, blk, n), axis=0, keepdims=True)
        hit = blk == first
        finite = jnp.abs(m) < float("inf")
        sel = jnp.where(hit, jnp.where(finite, 1.0, sel), sel)
        g = jnp.where(hit, NEG_INF, g)
    return sel


def _moba_prompt_kernel(slope_ref, qt_ref, k_ref, vt_ref, km_ref, o_ref, sel_ref, kaug_ref,
                        *bufs):
    u_refs, acc_refs = bufs[:2 * MOBA_CHAINS], bufs[2 * MOBA_CHAINS:]
    h = pl.program_id(0)
    i = pl.program_id(1)
    slope = slope_ref[h]
    blk = MOBA_BLOCK
    nb = sel_ref.shape[0]

    @pl.when(i == 0)
    def _():
        kc = lax.broadcasted_iota(jnp.int32, (blk, HEAD_DIM), 0)
        col = lax.broadcasted_iota(jnp.int32, (blk, HEAD_DIM), 1)
        kaug_ref[...] = jnp.where(col == 0, slope * kc.astype(F32), 0.0).astype(BF16)

    qt = qt_ref[...]
    gate_t = jnp.dot(km_ref[...].astype(BF16), qt, preferred_element_type=F32)
    sel_ref[...] = _top_mask_t(gate_t, i, MOBA_TOPK)

    one_row = lax.broadcasted_iota(jnp.int32, (HEAD_DIM, blk), 0) == 0
    qt_aug = jnp.concatenate([qt, jnp.where(one_row, 1.0, 0.0).astype(BF16)], axis=0)
    kaug = kaug_ref[...]

    def scores_t(j):
        js = pl.multiple_of(j * blk, blk)
        kj = jnp.concatenate([k_ref[pl.ds(js, blk), :], kaug], axis=1)
        return jnp.dot(kj, qt_aug, preferred_element_type=F32)

    def values_t(j, p):
        r = jnp.dot(vt_ref[j], p.astype(BF16), preferred_element_type=F32)
        return r[HEAD_DIM:HEAD_DIM + 1], r[:HEAD_DIM]

    def block_of(r, c):
        return jnp.minimum(r * MOBA_CHAINS + c, nb - 1)

    def issue_scores(r, buf_set):
        maxes = []
        for c in range(MOBA_CHAINS):
            u = scores_t(block_of(r, c))
            u_refs[buf_set * MOBA_CHAINS + c][...] = u
            maxes.append(jnp.max(u, axis=0, keepdims=True))
        return maxes

    u0 = scores_t(i)
    mx0 = issue_scores(0, 0)
    kc = lax.broadcasted_iota(jnp.int32, (blk, blk), 0)
    qr = lax.broadcasted_iota(jnp.int32, (blk, blk), 1)
    u0 = jnp.where(kc <= qr, u0, NEG_INF)
    m0 = jnp.max(u0, axis=0, keepdims=True)
    l0, acc0 = values_t(i, jnp.exp(u0 - m0))
    acc_refs[0][...] = acc0
    for c in range(1, MOBA_CHAINS):
        acc_refs[c][...] = jnp.zeros((HEAD_DIM, blk), F32)

    def chain_step(r, buf_set, chain, m, l, u_max):
        j_raw = r * MOBA_CHAINS + chain
        j = block_of(r, chain)
        picked = jnp.where(j_raw < i, sel_ref[pl.ds(j, 1), :], 0.0) > 0.0
        off = -slope * ((i - j) * blk).astype(F32)
        u_ref, acc_ref = u_refs[buf_set * MOBA_CHAINS + chain], acc_refs[chain]
        m_new = jnp.where(picked, jnp.maximum(m, u_max + off), m)
        x = u_ref[...] - jnp.where(picked, m_new - off, float("inf"))
        p = jnp.exp(x.astype(BF16))
        alpha = jnp.where(m_new == NEG_INF, 1.0, jnp.exp(m - m_new))
        lj, accj = values_t(j, p)
        acc_ref[...] = alpha * acc_ref[...] + accj
        return m_new, alpha * l + lj

    def body(rr, carry):
        ms, ls, mx = (list(x) for x in carry)
        for buf_set in range(2):
            r = 2 * rr + buf_set
            mx_next = issue_scores(r + 1, 1 - buf_set)
            for c in range(MOBA_CHAINS):
                ms[c], ls[c] = chain_step(r, buf_set, c, ms[c], ls[c], mx[c])
            mx = mx_next
        return tuple(ms), tuple(ls), tuple(mx)

    ms0 = (m0,) + (jnp.full((1, blk), NEG_INF, F32),) * (MOBA_CHAINS - 1)
    ls0 = (l0,) + (jnp.zeros((1, blk), F32),) * (MOBA_CHAINS - 1)
    per_iter = 2 * MOBA_CHAINS
    ms, ls, _ = lax.fori_loop(0, lax.div(i + per_iter - 1, per_iter), body, (ms0, ls0, tuple(mx0)))

    m_all = functools.reduce(jnp.maximum, ms)
    l = jnp.zeros((1, blk), F32)
    acc = jnp.zeros((HEAD_DIM, blk), F32)
    for c in range(MOBA_CHAINS):
        w = jnp.where(ms[c] == NEG_INF, 0.0, jnp.exp(ms[c] - m_all))
        l = l + w * ls[c]
        acc = acc + w * acc_refs[c][...]
    o_ref[...] = (acc / l).T.astype(o_ref.dtype)


def _moba_prompt(zb, qt, vt, kmean, slopes):
    s = zb.shape[0]
    nb = s // MOBA_BLOCK
    return pl.pallas_call(
        _moba_prompt_kernel,
        grid=(N_HEADS, nb),
        in_specs=[
            pl.BlockSpec(memory_space=pltpu.SMEM),
            pl.BlockSpec((None, HEAD_DIM, MOBA_BLOCK), lambda h, i: (i, h, 0)),
            pl.BlockSpec((s, HEAD_DIM), lambda h, i: (0, CH_KA * N_HEADS + h)),
            pl.BlockSpec((nb, None, V_AUG_ROWS, MOBA_BLOCK), lambda h, i: (0, h, 0, 0)),
            pl.BlockSpec((nb, HEAD_DIM), lambda h, i: (0, h)),
        ],
        out_specs=pl.BlockSpec((MOBA_BLOCK, HEAD_DIM), lambda h, i: (i, h)),
        out_shape=jax.ShapeDtypeStruct((s, WIDTH), BF16),
        scratch_shapes=[
            pltpu.VMEM((nb, MOBA_BLOCK), F32),
            pltpu.VMEM((MOBA_BLOCK, HEAD_DIM), BF16),
        ] + [pltpu.VMEM((MOBA_BLOCK, MOBA_BLOCK), F32)] * (2 * MOBA_CHAINS)
          + [pltpu.VMEM((HEAD_DIM, MOBA_BLOCK), F32)] * MOBA_CHAINS,
        compiler_params=_params("parallel", "arbitrary"),
        name="moba_prompt",
    )(slopes, qt, zb, vt, kmean)


PAGES_PER_STEP = SUBLANES * PAGES_PER_BLOCK


def _kmean_sample_kernel(pt_ref, *refs):
    del pt_ref
    page_refs, o_ref = refs[:-1], refs[-1]
    for n in range(len(page_refs) // PAGES_PER_BLOCK):
        tot = None
        for r in range(PAGES_PER_BLOCK):
            part = jnp.sum(page_refs[n * PAGES_PER_BLOCK + r][...], axis=0)
            tot = part if tot is None else tot + part
        o_ref[n] = tot * (1.0 / MOBA_BLOCK)


def _kmean_sample(cache_k, page_table):
    b, n_pages = page_table.shape
    n_full = n_pages // PAGES_PER_BLOCK
    pps = PAGES_PER_STEP if n_pages % PAGES_PER_STEP == 0 else n_pages
    bps = pps // PAGES_PER_BLOCK

    def page_spec(r):
        return pl.BlockSpec((None, PAGE_SIZE, N_HEADS, HEAD_DIM),
                            lambda i, s, pt: (pt[i, s * pps + r], 0, 0, 0))

    grid_spec = pltpu.PrefetchScalarGridSpec(
        num_scalar_prefetch=1,
        grid=(b, n_pages // pps),
        in_specs=[page_spec(r) for r in range(pps)],
        out_specs=pl.BlockSpec((None, bps, N_HEADS, HEAD_DIM), lambda i, s, pt: (i, s, 0, 0)),
    )
    return pl.pallas_call(
        _kmean_sample_kernel,
        grid_spec=grid_spec,
        out_shape=jax.ShapeDtypeStruct((b, n_full, N_HEADS, HEAD_DIM), F32),
        compiler_params=_params("parallel", "arbitrary"),
        name="kmean_sample",
    )(page_table, *([cache_k] * pps))


def _gate_sample_kernel(z_ref, km_ref, o_ref):
    rows = z_ref.shape[0]
    n = km_ref.shape[0]
    lane = lax.broadcasted_iota(jnp.int32, (rows, LANES), 1)
    blk = lax.broadcasted_iota(jnp.int32, (rows, n), 1)
    out = jnp.zeros((rows, LANES), jnp.int32)
    for h in range(N_HEADS):
        c0 = (CH_QA * N_HEADS + h) * HEAD_DIM
        q = z_ref[:, c0:c0 + HEAD_DIM].astype(BF16)
        kmh = km_ref[:, h, :].astype(BF16)
        g = _dot_nt(q, kmh)
        for r in range(MOBA_TOPK):
            m = jnp.max(g, axis=1, keepdims=True)
            first = jnp.min(jnp.where(g == m, blk, n), axis=1, keepdims=True)
            out = jnp.where(lane == h * MOBA_TOPK + r, first, out)
            g = jnp.where(blk == first, NEG_INF, g)
    o_ref[...] = out


def _gate_sample(z3, kmean_s):
    b, rows, _ = z3.shape
    n = kmean_s.shape[1]
    return pl.pallas_call(
        _gate_sample_kernel,
        grid=(b,),
        in_specs=[
            pl.BlockSpec((None, rows, IN_WIDTH), lambda i: (i, 0, 0)),
            pl.BlockSpec((None, n, N_HEADS, HEAD_DIM), lambda i: (i, 0, 0, 0)),
        ],
        out_specs=pl.BlockSpec((None, rows, LANES), lambda i: (i, 0, 0)),
        out_shape=jax.ShapeDtypeStruct((b, rows, LANES), jnp.int32),
        compiler_params=_params("parallel"),
        name="gate_sample",
    )(z3, kmean_s)


def _attn_sample_kernel(pt_ref, idx_ref, slope_ref, q_ref, kn_ref, vn_ref, ck_hbm, cv_hbm, o_ref,
                        kbuf, vbuf, sem, *, t, past):
    per_tok = MOBA_TOPK * PAGES_PER_BLOCK
    b = pl.program_id(0)
    h = pl.program_id(1)
    step = b * N_HEADS + h
    slot = lax.rem(step, 2)

    def copies(bb, hh, sl):
        out = []
        for tt in range(t):
            for r in range(MOBA_TOPK):
                blk = idx_ref[((bb * t + tt) * N_HEADS + hh) * MOBA_TOPK + r]
                for pg in range(PAGES_PER_BLOCK):
                    phys = pt_ref[bb, blk * PAGES_PER_BLOCK + pg]
                    j = tt * per_tok + r * PAGES_PER_BLOCK + pg
                    out.append(pltpu.make_async_copy(ck_hbm.at[phys, :, hh, :], kbuf.at[sl, j],
                                                     sem.at[sl]))
                    out.append(pltpu.make_async_copy(cv_hbm.at[phys, :, hh, :], vbuf.at[sl, j],
                                                     sem.at[sl]))
        return out

    @pl.when(step == 0)
    def _():
        for cp in copies(b, h, slot):
            cp.start()

    @pl.when(step + 1 < pl.num_programs(0) * N_HEADS)
    def _():
        last_head = h == N_HEADS - 1
        for cp in copies(jnp.where(last_head, b + 1, b), jnp.where(last_head, 0, h + 1), 1 - slot):
            cp.start()

    for cp in copies(b, h, slot):
        cp.wait()

    k_refs = [kbuf.at[slot, j] for j in range(t * per_tok)]
    v_refs = [vbuf.at[slot, j] for j in range(t * per_tok)]
    slope = slope_ref[h]
    rows = q_ref.shape[0]
    q = q_ref[...].astype(BF16)
    zpad = jnp.zeros((LANES - rows, HEAD_DIM), F32)
    kn = jnp.concatenate([kn_ref[...], zpad], axis=0)
    vn = jnp.concatenate([vn_ref[...], zpad], axis=0)
    lane_blk = lax.broadcasted_iota(jnp.int32, (1, MOBA_BLOCK), 1)
    lane_new = lax.broadcasted_iota(jnp.int32, (1, LANES), 1)
    scores = []
    for tt in range(t):
        ks = jnp.concatenate([k_refs[tt * per_tok + r][...] for r in range(per_tok)] + [kn], axis=0)
        scores.append(_dot_nt(q, ks.astype(BF16))[tt:tt + 1])
    probs = []
    for tt in range(t):
        dist = []
        for r in range(MOBA_TOPK):
            blk = idx_ref[((b * t + tt) * N_HEADS + h) * MOBA_TOPK + r]
            dist.append((past + tt - blk * MOBA_BLOCK - lane_blk).astype(F32))
        dist.append((tt - lane_new).astype(F32))
        valid = jnp.concatenate([jnp.ones((1, MOBA_TOPK * MOBA_BLOCK), jnp.int32),
                                 (lane_new <= tt).astype(jnp.int32)], axis=1)
        s = jnp.where(valid > 0, scores[tt] - slope * jnp.concatenate(dist, axis=1), NEG_INF)
        p = jnp.exp(s - jnp.max(s, axis=1, keepdims=True))
        probs.append((p, jnp.sum(p, axis=1, keepdims=True)))
    for tt in range(t):
        p, l = probs[tt]
        vs = jnp.concatenate([v_refs[tt * per_tok + r][...] for r in range(per_tok)] + [vn], axis=0)
        p8 = jnp.broadcast_to(p, (SUBLANES, p.shape[1])).astype(BF16)
        o = jnp.dot(p8, vs.astype(BF16), preferred_element_type=F32)[0:1]
        o_ref[tt:tt + 1, :] = o / l
    if rows > t:
        o_ref[t:, :] = jnp.zeros((rows - t, HEAD_DIM), F32)


def _attn_sample(z3, cache_k, cache_v, page_table, idx_flat, slopes, t, past):
    b, rows, _ = z3.shape
    n_sel = t * MOBA_TOPK * PAGES_PER_BLOCK

    def tok_spec(ch):
        return pl.BlockSpec((None, rows, HEAD_DIM), lambda i, h, pt, idx: (i, 0, ch * N_HEADS + h))

    grid_spec = pltpu.PrefetchScalarGridSpec(
        num_scalar_prefetch=2,
        grid=(b, N_HEADS),
        in_specs=[pl.BlockSpec(memory_space=pltpu.SMEM), tok_spec(CH_QA), tok_spec(CH_KA),
                  tok_spec(CH_VA), pl.BlockSpec(memory_space=pl.ANY),
                  pl.BlockSpec(memory_space=pl.ANY)],
        out_specs=pl.BlockSpec((None, rows, HEAD_DIM), lambda i, h, pt, idx: (i, 0, h)),
        scratch_shapes=[
            pltpu.VMEM((2, n_sel, PAGE_SIZE, HEAD_DIM), F32),
            pltpu.VMEM((2, n_sel, PAGE_SIZE, HEAD_DIM), F32),
            pltpu.SemaphoreType.DMA((2,)),
        ],
    )
    return pl.pallas_call(
        functools.partial(_attn_sample_kernel, t=t, past=past),
        grid_spec=grid_spec,
        out_shape=jax.ShapeDtypeStruct((b, rows, WIDTH), F32),
        compiler_params=_params("arbitrary", "arbitrary"),
        name="attn_sample",
    )(page_table, idx_flat, slopes, z3, z3, z3, cache_k, cache_v)


def _outproj_kernel(x_ref, gate_ref, ret_ref, att_ref, wr_ref, wa_ref, g_ref, b_ref, o_ref):
    y = jnp.dot(ret_ref[...].astype(BF16), wr_ref[...], preferred_element_type=F32)
    y = y + jnp.dot(att_ref[...].astype(BF16), wa_ref[...], preferred_element_type=F32)
    r = ALPHA * x_ref[...] + gate_ref[...] * y
    o_ref[...] = _layernorm(r, g_ref[...], b_ref[...])


def _outproj(x, mod, ret, att, w_o, ln_g, ln_b, tm):
    rows = x.shape[0]
    return pl.pallas_call(
        _outproj_kernel,
        grid=(rows // tm,),
        in_specs=[
            pl.BlockSpec((tm, D_MODEL), lambda i: (i, 0)),
            _mod_spec(mod, tm, 1, 2),
            pl.BlockSpec((tm, WIDTH), lambda i: (i, 0)),
            pl.BlockSpec((tm, WIDTH), lambda i: (i, 0)),
            pl.BlockSpec((WIDTH, D_MODEL), lambda i: (0, 0)),
            pl.BlockSpec((WIDTH, D_MODEL), lambda i: (1, 0)),
            pl.BlockSpec((1, D_MODEL), lambda i: (0, 0)),
            pl.BlockSpec((1, D_MODEL), lambda i: (0, 0)),
        ],
        out_specs=pl.BlockSpec((tm, D_MODEL), lambda i: (i, 0)),
        out_shape=jax.ShapeDtypeStruct((rows, D_MODEL), F32),
        compiler_params=_params("parallel"),
        name="outproj",
    )(x, mod, ret, att, w_o, w_o, ln_g[1:2], ln_b[1:2])


def _row_tile(rows, want):
    return want if rows % want == 0 else rows


def kernel(x_prompt, x_sample, cache_k, cache_v, state_ret, page_table, c_prompt, c_sample,
           w_ada, b_ada, w_ffn1_in, w_ffn1_out, w_ffn2_in, w_ffn2_out, w_in, w_o, ln_g, ln_b):
    assert w_ada.shape[0] == DEPTH and x_prompt.shape[0] == 1
    seq = x_prompt.shape[1]
    bd, t, _ = x_sample.shape
    n_pages = page_table.shape[1]
    past = n_pages * PAGE_SIZE
    assert seq % MOBA_BLOCK == 0 and n_pages % PAGES_PER_BLOCK == 0
    assert n_pages // PAGES_PER_BLOCK >= MOBA_TOPK and t <= SUBLANES

    hs = jnp.arange(N_HEADS, dtype=F32)
    log_gamma = jnp.log1p(-jnp.exp2(-5.0 - hs))
    slopes = jnp.exp2(-8.0 * (hs + 1.0) / N_HEADS)

    w1i, w1o = w_ffn1_in[0].astype(BF16), w_ffn1_out[0].astype(BF16)
    w2i, w2o = w_ffn2_in[0].astype(BF16), w_ffn2_out[0].astype(BF16)
    wi, wo = w_in[0].astype(BF16), w_o[0].astype(BF16)
    g_ln, b_ln = ln_g[0], ln_b[0]

    pad = (-(1 + bd)) % SUBLANES
    c_all = jnp.concatenate([c_prompt, c_sample, jnp.zeros((pad, D_MODEL), F32)], axis=0)
    mod = _adaln(c_all, w_ada[0], b_ada[0])
    mod_p = mod[0:1]
    mod_s = jnp.repeat(mod[1:1 + bd], t, axis=0)

    xp = x_prompt[0]
    tm_p = _row_tile(seq, 512)
    tf = 512
    xp = _ffn(xp, mod_p, 0, w1i, w1o, g_ln, b_ln, tm_p, tf)
    zb, g_r, k_a, v_a, qt, vt = _inproj(xp, mod_p, wi, tm_p, BF16, True)
    ret, state_p = _ret_prompt(zb, g_r, log_gamma, MOBA_BLOCK)
    kmean_p = _kmean_prompt(k_a)
    att = _moba_prompt(zb, qt, vt, kmean_p, slopes)
    xp = _outproj(xp, mod_p, ret, att, wo, g_ln, b_ln, tm_p)
    xp = _ffn(xp, mod_p, 2, w2i, w2o, g_ln, b_ln, tm_p, tf)

    rows_s = bd * t
    xs = x_sample.reshape(rows_s, D_MODEL)
    xs = _ffn(xs, mod_s, 0, w1i, w1o, g_ln, b_ln, rows_s, tf)
    zs, gs, ks, vs = _inproj(xs, mod_s, wi, rows_s, F32, False)
    tpad = ((0, 0), (0, SUBLANES - t), (0, 0))
    zs3 = jnp.pad(zs.reshape(bd, t, IN_WIDTH), tpad)
    gs3 = jnp.pad(gs.reshape(bd, t, WIDTH), tpad)
    ret_s, state_s = _ret_sample(zs3, gs3, state_ret[0], log_gamma, t)
    kmean_s = _kmean_sample(cache_k[0], page_table)
    idx = _gate_sample(zs3, kmean_s)[:, :t, :N_HEADS * MOBA_TOPK].reshape(-1)
    att_s = _attn_sample(zs3, cache_k[0], cache_v[0], page_table, idx, slopes, t, past)
    xs = _outproj(xs, mod_s, ret_s[:, :t].reshape(rows_s, WIDTH),
                  att_s[:, :t].reshape(rows_s, WIDTH), wo, g_ln, b_ln, rows_s)
    xs = _ffn(xs, mod_s, 2, w2i, w2o, g_ln, b_ln, rows_s, tf)

    hd = (N_HEADS, HEAD_DIM)
    return (xp[None], xs.reshape(bd, t, D_MODEL),
            k_a.reshape(1, 1, seq, *hd), v_a.reshape(1, 1, seq, *hd), state_p[None, None],
            ks.reshape(1, bd, t, *hd), vs.reshape(1, bd, t, *hd), state_s[None])
```

```python
import functools

import jax
import jax.numpy as jnp
from jax import lax
from jax.experimental import pallas as pl
from jax.experimental.pallas import tpu as pltpu

F32 = jnp.float32
BF16 = jnp.bfloat16

D_MODEL = 2048
N_HEADS = 8
HEAD_DIM = 128
WIDTH = N_HEADS * HEAD_DIM
N_CHUNKS = 7
IN_WIDTH = N_CHUNKS * WIDTH
D_FF = 5632
MOBA_BLOCK = 256
MOBA_TOPK = 3
PAGE_SIZE = 128
PAGES_PER_BLOCK = MOBA_BLOCK // PAGE_SIZE
LN_EPS = 1e-5
DEPTH = 1
ALPHA = (2 * DEPTH) ** 0.25
FFN_RESIDUAL = 0.5
QK_SCALE = HEAD_DIM ** -0.5
NEG_INF = float("-inf")
SUBLANES = 8
BF16_SUBLANES = 16
LANES = 128
V_AUG_ROWS = HEAD_DIM + BF16_SUBLANES
MOBA_CHAINS = 4

VMEM_LIMIT_BYTES = 56 * 1024 * 1024

CH_QR, CH_KR, CH_VR, CH_GR, CH_QA, CH_KA, CH_VA = range(N_CHUNKS)


def _params(*sem):
    return pltpu.CompilerParams(dimension_semantics=sem, vmem_limit_bytes=VMEM_LIMIT_BYTES)


def _silu(x):
    return x * jax.nn.sigmoid(x)


def _layernorm(r, g, b):
    mu = jnp.mean(r, axis=-1, keepdims=True)
    xc = r - mu
    var = jnp.mean(xc * xc, axis=-1, keepdims=True)
    return xc * lax.rsqrt(var + LN_EPS) * g + b


def _dot_nt(a, b):
    return lax.dot_general(a, b, (((1,), (1,)), ((), ())), preferred_element_type=F32)


def _dot_tn(a, b):
    return lax.dot_general(a, b, (((0,), (0,)), ((), ())), preferred_element_type=F32)


def _adaln_kernel(c_ref, w_ref, b_ref, o_ref):
    a = _silu(c_ref[...]).astype(BF16)
    o_ref[...] = jnp.dot(a, w_ref[...].astype(BF16), preferred_element_type=F32) + b_ref[...]


def _adaln(c_all, w_ada, b_ada):
    rows = c_all.shape[0]
    n = w_ada.shape[1]
    tn = 1024
    return pl.pallas_call(
        _adaln_kernel,
        grid=(n // tn,),
        in_specs=[
            pl.BlockSpec((rows, D_MODEL), lambda j: (0, 0)),
            pl.BlockSpec((D_MODEL, tn), lambda j: (0, j)),
            pl.BlockSpec((1, tn), lambda j: (0, j)),
        ],
        out_specs=pl.BlockSpec((rows, tn), lambda j: (0, j)),
        out_shape=jax.ShapeDtypeStruct((rows, n), F32),
        compiler_params=_params("arbitrary"),
        name="adaln",
    )(c_all, w_ada, b_ada.reshape(1, n))


def _mod_spec(mod, tm, sub, part):
    col = sub * 3 + part
    if mod.shape[0] == 1:
        return pl.BlockSpec((1, D_MODEL), lambda i, *_: (0, col))
    return pl.BlockSpec((tm, D_MODEL), lambda i, *_: (i, col))


def _ffn_kernel(x_ref, shift_ref, scale_ref, gate_ref, wa_ref, wu_ref, wo_ref, g_ref, b_ref,
                o_ref, xm_ref, acc_ref):
    f = pl.program_id(1)

    @pl.when(f == 0)
    def _():
        xm_ref[...] = (x_ref[...] * (1.0 + scale_ref[...]) + shift_ref[...]).astype(BF16)
        acc_ref[...] = jnp.zeros_like(acc_ref)

    xm = xm_ref[...]
    a = jnp.dot(xm, wa_ref[...], preferred_element_type=F32)
    u = jnp.dot(xm, wu_ref[...], preferred_element_type=F32)
    h = (_silu(a) * u).astype(BF16)
    acc_ref[...] += jnp.dot(h, wo_ref[...], preferred_element_type=F32)

    @pl.when(f == pl.num_programs(1) - 1)
    def _():
        r = ALPHA * x_ref[...] + FFN_RESIDUAL * gate_ref[...] * acc_ref[...]
        o_ref[...] = _layernorm(r, g_ref[...], b_ref[...])


def _ffn(x, mod, sub, w_in, w_out, ln_g, ln_b, tm, tf):
    rows = x.shape[0]
    nf = D_FF // tf
    return pl.pallas_call(
        _ffn_kernel,
        grid=(rows // tm, nf),
        in_specs=[
            pl.BlockSpec((tm, D_MODEL), lambda i, f: (i, 0)),
            _mod_spec(mod, tm, sub, 0),
            _mod_spec(mod, tm, sub, 1),
            _mod_spec(mod, tm, sub, 2),
            pl.BlockSpec((D_MODEL, tf), lambda i, f: (0, f)),
            pl.BlockSpec((D_MODEL, tf), lambda i, f: (0, f + nf)),
            pl.BlockSpec((tf, D_MODEL), lambda i, f: (f, 0)),
            pl.BlockSpec((1, D_MODEL), lambda i, f: (0, 0)),
            pl.BlockSpec((1, D_MODEL), lambda i, f: (0, 0)),
        ],
        out_specs=pl.BlockSpec((tm, D_MODEL), lambda i, f: (i, 0)),
        out_shape=jax.ShapeDtypeStruct((rows, D_MODEL), F32),
        scratch_shapes=[pltpu.VMEM((tm, D_MODEL), BF16), pltpu.VMEM((tm, D_MODEL), F32)],
        compiler_params=_params("parallel", "arbitrary"),
        name="ffn",
    )(x, mod, mod, mod, w_in, w_in, w_out, ln_g[sub:sub + 1], ln_b[sub:sub + 1])


def _inproj_kernel(x_ref, shift_ref, scale_ref, w_ref, z_ref, g_ref, k_ref, v_ref, *rest,
                   emit_transposed):
    if emit_transposed:
        qt_ref, vt_ref, km_ref, xm_ref = rest
    else:
        (xm_ref,) = rest
    j = pl.program_id(1)

    @pl.when(j == 0)
    def _():
        xm_ref[...] = (x_ref[...] * (1.0 + scale_ref[...]) + shift_ref[...]).astype(BF16)

    z = jnp.dot(xm_ref[...], w_ref[...], preferred_element_type=F32)

    def put_transposed(ref, val):
        zt = val.T.astype(BF16)
        for n in range(ref.shape[0]):
            blk = zt[:, n * MOBA_BLOCK:(n + 1) * MOBA_BLOCK]
            if ref.ndim == 3:
                ref[n] = blk
            else:
                for h in range(N_HEADS):
                    ref[n, h, :HEAD_DIM, :] = blk[h * HEAD_DIM:(h + 1) * HEAD_DIM]
                    ref[n, h, HEAD_DIM:, :] = jnp.ones((V_AUG_ROWS - HEAD_DIM, MOBA_BLOCK), BF16)

    @pl.when(j != CH_QA)
    def _():
        z_ref[...] = z.astype(z_ref.dtype)

    @pl.when(j == CH_QA)
    def _():
        zs = z * QK_SCALE
        z_ref[...] = zs.astype(z_ref.dtype)
        if emit_transposed:
            put_transposed(qt_ref, zs)

    @pl.when(j == CH_GR)
    def _():
        g_ref[...] = z

    @pl.when(j == CH_KA)
    def _():
        k_ref[...] = z
        if emit_transposed:
            n = km_ref.shape[0]
            km_ref[...] = jnp.sum(z.reshape(n, MOBA_BLOCK, WIDTH), axis=1) * (1.0 / MOBA_BLOCK)

    @pl.when(j == CH_VA)
    def _():
        v_ref[...] = z
        if emit_transposed:
            put_transposed(vt_ref, z)


def _inproj(x, mod, w_in, tm, z_dtype, emit_transposed):
    rows = x.shape[0]
    out_shape = [
        jax.ShapeDtypeStruct((rows, IN_WIDTH), z_dtype),
        jax.ShapeDtypeStruct((rows, WIDTH), F32),
        jax.ShapeDtypeStruct((rows, WIDTH), F32),
        jax.ShapeDtypeStruct((rows, WIDTH), F32),
    ]
    out_specs = [
        pl.BlockSpec((tm, WIDTH), lambda i, j: (i, j)),
        pl.BlockSpec((tm, WIDTH), lambda i, j: (i, 0)),
        pl.BlockSpec((tm, WIDTH), lambda i, j: (i, 0)),
        pl.BlockSpec((tm, WIDTH), lambda i, j: (i, 0)),
    ]
    if emit_transposed:
        nb = tm // MOBA_BLOCK
        out_shape.append(jax.ShapeDtypeStruct((rows // MOBA_BLOCK, WIDTH, MOBA_BLOCK), BF16))
        out_specs.append(pl.BlockSpec((nb, WIDTH, MOBA_BLOCK), lambda i, j: (i, 0, 0)))
        out_shape.append(jax.ShapeDtypeStruct(
            (rows // MOBA_BLOCK, N_HEADS, V_AUG_ROWS, MOBA_BLOCK), BF16))
        out_specs.append(pl.BlockSpec((nb, N_HEADS, V_AUG_ROWS, MOBA_BLOCK),
                                      lambda i, j: (i, 0, 0, 0)))
        out_shape.append(jax.ShapeDtypeStruct((rows // tm, nb, WIDTH), F32))
        out_specs.append(pl.BlockSpec((None, nb, WIDTH), lambda i, j: (i, 0, 0)))
    return pl.pallas_call(
        functools.partial(_inproj_kernel, emit_transposed=emit_transposed),
        grid=(rows // tm, N_CHUNKS),
        in_specs=[
            pl.BlockSpec((tm, D_MODEL), lambda i, j: (i, 0)),
            _mod_spec(mod, tm, 1, 0),
            _mod_spec(mod, tm, 1, 1),
            pl.BlockSpec((D_MODEL, WIDTH), lambda i, j: (0, j)),
        ],
        out_specs=out_specs,
        out_shape=out_shape,
        scratch_shapes=[pltpu.VMEM((tm, D_MODEL), BF16)],
        compiler_params=_params("parallel", "arbitrary"),
        name="inproj",
    )(x, mod, mod, w_in)


def _decay_tables(lg, c, n):
    row = lax.broadcasted_iota(jnp.int32, (n, n), 0)
    col = lax.broadcasted_iota(jnp.int32, (n, n), 1)
    diff = (row - col).astype(F32)
    dmat = jnp.where(diff >= 0, jnp.exp(lg * jnp.maximum(diff, 0.0)), 0.0) * QK_SCALE
    i = lax.broadcasted_iota(jnp.int32, (n, HEAD_DIM), 0).astype(F32)
    cross = jnp.exp(lg * (i + 1.0))
    wtail = jnp.exp(lg * (c - 1.0 - i)) * QK_SCALE
    return dmat, cross, wtail


def _groupnorm_gate(o, g):
    mu = jnp.mean(o, axis=-1, keepdims=True)
    oc = o - mu
    var = jnp.mean(oc * oc, axis=-1, keepdims=True)
    return _silu(g) * (oc * lax.rsqrt(var + LN_EPS))


def _retention_heads(heads):
    first = []
    for q, k, kf, v, state, _, _, wtail, _ in heads:
        first.append((_dot_nt(q, k),
                      jnp.dot(q, state.astype(BF16), preferred_element_type=F32),
                      _dot_tn((kf * wtail).astype(BF16), v)))
    out = []
    for (_, _, _, v, state, dmat, cross, _, gc), (s, o_cross, kv) in zip(heads, first):
        o = jnp.dot((s * dmat).astype(BF16), v, preferred_element_type=F32) + o_cross * cross
        out.append((o, gc * state + kv))
    return out


def _ret_prompt_kernel(lg_ref, q_ref, k_ref, v_ref, g_ref, o_ref, state_ref,
                       dmat_ref, cross_ref, wtail_ref, *, chunk):
    c = pl.program_id(0)

    @pl.when(c == 0)
    def _():
        for h in range(N_HEADS):
            dmat_ref[h], cross_ref[h], wtail_ref[h] = _decay_tables(lg_ref[h], chunk, chunk)
        state_ref[...] = jnp.zeros_like(state_ref)

    heads = []
    for h in range(N_HEADS):
        cols = slice(h * HEAD_DIM, (h + 1) * HEAD_DIM)
        k = k_ref[:, cols]
        gc = jnp.exp(jnp.full((HEAD_DIM, HEAD_DIM), lg_ref[h] * chunk, F32))
        heads.append((q_ref[:, cols], k, k.astype(F32), v_ref[:, cols], state_ref[h],
                      dmat_ref[h], cross_ref[h], wtail_ref[h], gc))
    for h, (o, state) in enumerate(_retention_heads(heads)):
        cols = slice(h * HEAD_DIM, (h + 1) * HEAD_DIM)
        state_ref[h] = state
        o_ref[:, cols] = _groupnorm_gate(o, g_ref[:, cols]).astype(o_ref.dtype)


def _ret_prompt(zb, g, log_gamma, chunk):
    s = zb.shape[0]
    return pl.pallas_call(
        functools.partial(_ret_prompt_kernel, chunk=chunk),
        grid=(s // chunk,),
        in_specs=[
            pl.BlockSpec(memory_space=pltpu.SMEM),
            pl.BlockSpec((chunk, WIDTH), lambda c: (c, CH_QR)),
            pl.BlockSpec((chunk, WIDTH), lambda c: (c, CH_KR)),
            pl.BlockSpec((chunk, WIDTH), lambda c: (c, CH_VR)),
            pl.BlockSpec((chunk, WIDTH), lambda c: (c, 0)),
        ],
        out_specs=[
            pl.BlockSpec((chunk, WIDTH), lambda c: (c, 0)),
            pl.BlockSpec((N_HEADS, HEAD_DIM, HEAD_DIM), lambda c: (0, 0, 0)),
        ],
        out_shape=[
            jax.ShapeDtypeStruct((s, WIDTH), BF16),
            jax.ShapeDtypeStruct((N_HEADS, HEAD_DIM, HEAD_DIM), F32),
        ],
        scratch_shapes=[
            pltpu.VMEM((N_HEADS, chunk, chunk), F32),
            pltpu.VMEM((N_HEADS, chunk, HEAD_DIM), F32),
            pltpu.VMEM((N_HEADS, chunk, HEAD_DIM), F32),
        ],
        compiler_params=_params("arbitrary"),
        name="ret_prompt",
    )(log_gamma, zb, zb, zb, g)


def _ret_sample_kernel(lg_ref, z_ref, g_ref, st_ref, o_ref, sn_ref, *, t):
    n = z_ref.shape[0]
    heads = []
    for h in range(N_HEADS):
        lg = lg_ref[h]
        dmat, cross, wtail = _decay_tables(lg, t, n)

        def head(ch, h=h):
            c0 = (ch * N_HEADS + h) * HEAD_DIM
            return z_ref[:, c0:c0 + HEAD_DIM]

        kf = head(CH_KR)
        gc = jnp.exp(jnp.full((HEAD_DIM, HEAD_DIM), lg * t, F32))
        heads.append((head(CH_QR).astype(BF16), kf.astype(BF16), kf, head(CH_VR).astype(BF16),
                      st_ref[h], dmat, cross, wtail, gc))
    for h, (o, state) in enumerate(_retention_heads(heads)):
        sn_ref[h] = state
        gh = g_ref[:, h * HEAD_DIM:(h + 1) * HEAD_DIM]
        o_ref[:, h * HEAD_DIM:(h + 1) * HEAD_DIM] = _groupnorm_gate(o, gh)


def _ret_sample(z3, g3, state, log_gamma, t):
    b, n, _ = z3.shape
    return pl.pallas_call(
        functools.partial(_ret_sample_kernel, t=t),
        grid=(b,),
        in_specs=[
            pl.BlockSpec(memory_space=pltpu.SMEM),
            pl.BlockSpec((None, n, IN_WIDTH), lambda i: (i, 0, 0)),
            pl.BlockSpec((None, n, WIDTH), lambda i: (i, 0, 0)),
            pl.BlockSpec((None, N_HEADS, HEAD_DIM, HEAD_DIM), lambda i: (i, 0, 0, 0)),
        ],
        out_specs=[
            pl.BlockSpec((None, n, WIDTH), lambda i: (i, 0, 0)),
            pl.BlockSpec((None, N_HEADS, HEAD_DIM, HEAD_DIM), lambda i: (i, 0, 0, 0)),
        ],
        out_shape=[
            jax.ShapeDtypeStruct((b, n, WIDTH), F32),
            jax.ShapeDtypeStruct(state.shape, F32),
        ],
        compiler_params=_params("parallel"),
        name="ret_sample",
    )(log_gamma, z3, g3, state)


def _top_mask_t(gate_t, n_valid, top):
    n = gate_t.shape[0]
    blk = lax.broadcasted_iota(jnp.int32, gate_t.shape, 0)
    g = jnp.where(blk < n_valid, gate_t, NEG_INF)
    sel = jnp.zeros(gate_t.shape, F32)
    for _ in range(top):
        m = jnp.max(g, axis=0, keepdims=True)
        first = jnp.min(jnp.where(g == m, blk, n), axis=0, keepdims=True)
        hit = blk == first
        finite = jnp.abs(m) < float("inf")
        sel = jnp.where(hit, jnp.where(finite, 1.0, sel), sel)
        g = jnp.where(hit, NEG_INF, g)
    return sel


def _moba_prompt_kernel(pt_ref, slope_ref, qt_ref, k_ref, vt_ref, km_ref, cache_hbm,
                        o_ref, kms_ref, sel_ref, kaug_ref, pbuf, psem, *bufs):
    u_refs, acc_refs = bufs[:2 * MOBA_CHAINS], bufs[2 * MOBA_CHAINS:]
    h = pl.program_id(0)
    i = pl.program_id(1)
    slope = slope_ref[h]
    blk = MOBA_BLOCK
    nb = sel_ref.shape[0]

    step = h * nb + i
    slot = lax.rem(step, 2)
    pages_per_step = pbuf.shape[1]
    n_pages = pt_ref.shape[1]

    def page_copies(st, sl):
        first = st * pages_per_step
        seq = first // n_pages
        page0 = first - seq * n_pages
        return [pltpu.make_async_copy(cache_hbm.at[pt_ref[seq, page0 + r]], pbuf.at[sl, r],
                                      psem.at[sl]) for r in range(pages_per_step)]

    @pl.when(step == 0)
    def _():
        for cp in page_copies(step, slot):
            cp.start()

    @pl.when(step + 1 < pl.num_programs(0) * nb)
    def _():
        for cp in page_copies(step + 1, 1 - slot):
            cp.start()

    @pl.when(i == 0)
    def _():
        kc = lax.broadcasted_iota(jnp.int32, (blk, HEAD_DIM), 0)
        col = lax.broadcasted_iota(jnp.int32, (blk, HEAD_DIM), 1)
        kaug_ref[...] = jnp.where(col == 0, slope * kc.astype(F32), 0.0).astype(BF16)

    for cp in page_copies(step, slot):
        cp.wait()
    for n in range(pages_per_step // PAGES_PER_BLOCK):
        tot = None
        for r in range(PAGES_PER_BLOCK):
            part = jnp.sum(pbuf[slot, n * PAGES_PER_BLOCK + r], axis=0)
            tot = part if tot is None else tot + part
        kms_ref[n] = tot * (1.0 / MOBA_BLOCK)

    qt = qt_ref[...]
    gate_t = jnp.dot(km_ref[...].astype(BF16), qt, preferred_element_type=F32)
    sel_ref[...] = _top_mask_t(gate_t, i, MOBA_TOPK)

    one_row = lax.broadcasted_iota(jnp.int32, (HEAD_DIM, blk), 0) == 0
    qt_aug = jnp.concatenate([qt, jnp.where(one_row, 1.0, 0.0).astype(BF16)], axis=0)
    kaug = kaug_ref[...]

    def scores_t(j):
        js = pl.multiple_of(j * blk, blk)
        kj = jnp.concatenate([k_ref[pl.ds(js, blk), :], kaug], axis=1)
        return jnp.dot(kj, qt_aug, preferred_element_type=F32)

    def values_t(j, p):
        r = jnp.dot(vt_ref[j], p.astype(BF16), preferred_element_type=F32)
        return r[HEAD_DIM:HEAD_DIM + 1], r[:HEAD_DIM]

    def block_of(r, c):
        return jnp.minimum(r * MOBA_CHAINS + c, nb - 1)

    def issue_scores(r, buf_set):
        maxes = []
        for c in range(MOBA_CHAINS):
            u = scores_t(block_of(r, c))
            u_refs[buf_set * MOBA_CHAINS + c][...] = u
            maxes.append(jnp.max(u, axis=0, keepdims=True))
        return maxes

    u0 = scores_t(i)
    mx0 = issue_scores(0, 0)
    kc = lax.broadcasted_iota(jnp.int32, (blk, blk), 0)
    qr = lax.broadcasted_iota(jnp.int32, (blk, blk), 1)
    u0 = jnp.where(kc <= qr, u0, NEG_INF)
    m0 = jnp.max(u0, axis=0, keepdims=True)
    l0, acc0 = values_t(i, jnp.exp(u0 - m0))
    acc_refs[0][...] = acc0
    for c in range(1, MOBA_CHAINS):
        acc_refs[c][...] = jnp.zeros((HEAD_DIM, blk), F32)

    def chain_step(r, buf_set, chain, m, l, u_max):
        j_raw = r * MOBA_CHAINS + chain
        j = block_of(r, chain)
        picked = jnp.where(j_raw < i, sel_ref[pl.ds(j, 1), :], 0.0) > 0.0
        off = -slope * ((i - j) * blk).astype(F32)
        u_ref, acc_ref = u_refs[buf_set * MOBA_CHAINS + chain], acc_refs[chain]
        m_new = jnp.where(picked, jnp.maximum(m, u_max + off), m)
        x = u_ref[...] - jnp.where(picked, m_new - off, float("inf"))
        p = jnp.exp(x.astype(BF16))
        alpha = jnp.where(m_new == NEG_INF, 1.0, jnp.exp(m - m_new))
        lj, accj = values_t(j, p)
        acc_ref[...] = alpha * acc_ref[...] + accj
        return m_new, alpha * l + lj

    def body(rr, carry):
        ms, ls, mx = (list(x) for x in carry)
        for buf_set in range(2):
            r = 2 * rr + buf_set
            mx_next = issue_scores(r + 1, 1 - buf_set)
            for c in range(MOBA_CHAINS):
                ms[c], ls[c] = chain_step(r, buf_set, c, ms[c], ls[c], mx[c])
            mx = mx_next
        return tuple(ms), tuple(ls), tuple(mx)

    ms0 = (m0,) + (jnp.full((1, blk), NEG_INF, F32),) * (MOBA_CHAINS - 1)
    ls0 = (l0,) + (jnp.zeros((1, blk), F32),) * (MOBA_CHAINS - 1)
    per_iter = 2 * MOBA_CHAINS
    ms, ls, _ = lax.fori_loop(0, lax.div(i + per_iter - 1, per_iter), body, (ms0, ls0, tuple(mx0)))

    m_all = functools.reduce(jnp.maximum, ms)
    l = jnp.zeros((1, blk), F32)
    acc = jnp.zeros((HEAD_DIM, blk), F32)
    for c in range(MOBA_CHAINS):
        w = jnp.where(ms[c] == NEG_INF, 0.0, jnp.exp(ms[c] - m_all))
        l = l + w * ls[c]
        acc = acc + w * acc_refs[c][...]
    o_ref[...] = (acc / l).T.astype(o_ref.dtype)


def _moba_prompt(zb, qt, vt, kmean, slopes, cache_k, page_table):
    s = zb.shape[0]
    nb = s // MOBA_BLOCK
    n_seq, n_pages = page_table.shape
    n_steps = N_HEADS * nb
    pps = (n_seq * n_pages) // n_steps
    assert pps * n_steps == n_seq * n_pages and pps % PAGES_PER_BLOCK == 0 and n_pages % pps == 0
    bps = pps // PAGES_PER_BLOCK

    def kms_index(h, i, pt):
        first = (h * nb + i) * pps
        return (first // n_pages, (first % n_pages) // pps, 0, 0)

    grid_spec = pltpu.PrefetchScalarGridSpec(
        num_scalar_prefetch=1,
        grid=(N_HEADS, nb),
        in_specs=[
            pl.BlockSpec(memory_space=pltpu.SMEM),
            pl.BlockSpec((None, HEAD_DIM, MOBA_BLOCK), lambda h, i, pt: (i, h, 0)),
            pl.BlockSpec((s, HEAD_DIM), lambda h, i, pt: (0, CH_KA * N_HEADS + h)),
            pl.BlockSpec((nb, None, V_AUG_ROWS, MOBA_BLOCK), lambda h, i, pt: (0, h, 0, 0)),
            pl.BlockSpec((nb, HEAD_DIM), lambda h, i, pt: (0, h)),
            pl.BlockSpec(memory_space=pl.ANY),
        ],
        out_specs=[
            pl.BlockSpec((MOBA_BLOCK, HEAD_DIM), lambda h, i, pt: (i, h)),
            pl.BlockSpec((None, bps, N_HEADS, HEAD_DIM), kms_index),
        ],
        scratch_shapes=[
            pltpu.VMEM((nb, MOBA_BLOCK), F32),
            pltpu.VMEM((MOBA_BLOCK, HEAD_DIM), BF16),
            pltpu.VMEM((2, pps, PAGE_SIZE, N_HEADS, HEAD_DIM), F32),
            pltpu.SemaphoreType.DMA((2,)),
        ] + [pltpu.VMEM((MOBA_BLOCK, MOBA_BLOCK), F32)] * (2 * MOBA_CHAINS)
          + [pltpu.VMEM((HEAD_DIM, MOBA_BLOCK), F32)] * MOBA_CHAINS,
    )
    return pl.pallas_call(
        _moba_prompt_kernel,
        grid_spec=grid_spec,
        out_shape=[
            jax.ShapeDtypeStruct((s, WIDTH), BF16),
            jax.ShapeDtypeStruct((n_seq, n_pages // PAGES_PER_BLOCK, N_HEADS, HEAD_DIM), F32),
        ],
        compiler_params=_params("arbitrary", "arbitrary"),
        name="moba_prompt",
    )(page_table, slopes, qt, zb, vt, kmean, cache_k)


def _gate_sample_kernel(z_ref, km_ref, o_ref):
    rows = z_ref.shape[0]
    n = km_ref.shape[0]
    lane = lax.broadcasted_iota(jnp.int32, (rows, LANES), 1)
    blk = lax.broadcasted_iota(jnp.int32, (rows, n), 1)
    out = jnp.zeros((rows, LANES), jnp.int32)
    for h in range(N_HEADS):
        c0 = (CH_QA * N_HEADS + h) * HEAD_DIM
        q = z_ref[:, c0:c0 + HEAD_DIM].astype(BF16)
        kmh = km_ref[:, h, :].astype(BF16)
        g = _dot_nt(q, kmh)
        for r in range(MOBA_TOPK):
            m = jnp.max(g, axis=1, keepdims=True)
            first = jnp.min(jnp.where(g == m, blk, n), axis=1, keepdims=True)
            out = jnp.where(lane == h * MOBA_TOPK + r, first, out)
            g = jnp.where(blk == first, NEG_INF, g)
    o_ref[...] = out


def _gate_sample(z3, kmean_s):
    b, rows, _ = z3.shape
    n = kmean_s.shape[1]
    return pl.pallas_call(
        _gate_sample_kernel,
        grid=(b,),
        in_specs=[
            pl.BlockSpec((None, rows, IN_WIDTH), lambda i: (i, 0, 0)),
            pl.BlockSpec((None, n, N_HEADS, HEAD_DIM), lambda i: (i, 0, 0, 0)),
        ],
        out_specs=pl.BlockSpec((None, rows, LANES), lambda i: (i, 0, 0)),
        out_shape=jax.ShapeDtypeStruct((b, rows, LANES), jnp.int32),
        compiler_params=_params("parallel"),
        name="gate_sample",
    )(z3, kmean_s)


def _attn_sample_kernel(pt_ref, idx_ref, slope_ref, q_ref, kn_ref, vn_ref, ck_hbm, cv_hbm, o_ref,
                        kbuf, vbuf, sem, *, t, past):
    per_tok = MOBA_TOPK * PAGES_PER_BLOCK
    b = pl.program_id(0)
    h = pl.program_id(1)
    step = b * N_HEADS + h
    slot = lax.rem(step, 2)

    def copies(bb, hh, sl):
        out = []
        for tt in range(t):
            for r in range(MOBA_TOPK):
                blk = idx_ref[((bb * t + tt) * N_HEADS + hh) * MOBA_TOPK + r]
                for pg in range(PAGES_PER_BLOCK):
                    phys = pt_ref[bb, blk * PAGES_PER_BLOCK + pg]
                    j = tt * per_tok + r * PAGES_PER_BLOCK + pg
                    out.append(pltpu.make_async_copy(ck_hbm.at[phys, :, hh, :], kbuf.at[sl, j],
                                                     sem.at[sl]))
                    out.append(pltpu.make_async_copy(cv_hbm.at[phys, :, hh, :], vbuf.at[sl, j],
                                                     sem.at[sl]))
        return out

    @pl.when(step == 0)
    def _():
        for cp in copies(b, h, slot):
            cp.start()

    @pl.when(step + 1 < pl.num_programs(0) * N_HEADS)
    def _():
        last_head = h == N_HEADS - 1
        for cp in copies(jnp.where(last_head, b + 1, b), jnp.where(last_head, 0, h + 1), 1 - slot):
            cp.start()

    for cp in copies(b, h, slot):
        cp.wait()

    k_refs = [kbuf.at[slot, j] for j in range(t * per_tok)]
    v_refs = [vbuf.at[slot, j] for j in range(t * per_tok)]
    slope = slope_ref[h]
    rows = q_ref.shape[0]
    q = q_ref[...].astype(BF16)
    zpad = jnp.zeros((LANES - rows, HEAD_DIM), F32)
    kn = jnp.concatenate([kn_ref[...], zpad], axis=0)
    vn = jnp.concatenate([vn_ref[...], zpad], axis=0)
    lane_blk = lax.broadcasted_iota(jnp.int32, (1, MOBA_BLOCK), 1)
    lane_new = lax.broadcasted_iota(jnp.int32, (1, LANES), 1)
    scores = []
    for tt in range(t):
        ks = jnp.concatenate([k_refs[tt * per_tok + r][...] for r in range(per_tok)] + [kn], axis=0)
        scores.append(_dot_nt(q, ks.astype(BF16))[tt:tt + 1])
    probs = []
    for tt in range(t):
        dist = []
        for r in range(MOBA_TOPK):
            blk = idx_ref[((b * t + tt) * N_HEADS + h) * MOBA_TOPK + r]
            dist.append((past + tt - blk * MOBA_BLOCK - lane_blk).astype(F32))
        dist.append((tt - lane_new).astype(F32))
        valid = jnp.concatenate([jnp.ones((1, MOBA_TOPK * MOBA_BLOCK), jnp.int32),
                                 (lane_new <= tt).astype(jnp.int32)], axis=1)
        s = jnp.where(valid > 0, scores[tt] - slope * jnp.concatenate(dist, axis=1), NEG_INF)
        p = jnp.exp(s - jnp.max(s, axis=1, keepdims=True))
        probs.append((p, jnp.sum(p, axis=1, keepdims=True)))
    for tt in range(t):
        p, l = probs[tt]
        vs = jnp.concatenate([v_refs[tt * per_tok + r][...] for r in range(per_tok)] + [vn], axis=0)
        p8 = jnp.broadcast_to(p, (SUBLANES, p.shape[1])).astype(BF16)
        o = jnp.dot(p8, vs.astype(BF16), preferred_element_type=F32)[0:1]
        o_ref[tt:tt + 1, :] = o / l
    if rows > t:
        o_ref[t:, :] = jnp.zeros((rows - t, HEAD_DIM), F32)


def _attn_sample(z3, cache_k, cache_v, page_table, idx_flat, slopes, t, past):
    b, rows, _ = z3.shape
    n_sel = t * MOBA_TOPK * PAGES_PER_BLOCK

    def tok_spec(ch):
        return pl.BlockSpec((None, rows, HEAD_DIM), lambda i, h, pt, idx: (i, 0, ch * N_HEADS + h))

    grid_spec = pltpu.PrefetchScalarGridSpec(
        num_scalar_prefetch=2,
        grid=(b, N_HEADS),
        in_specs=[pl.BlockSpec(memory_space=pltpu.SMEM), tok_spec(CH_QA), tok_spec(CH_KA),
                  tok_spec(CH_VA), pl.BlockSpec(memory_space=pl.ANY),
                  pl.BlockSpec(memory_space=pl.ANY)],
        out_specs=pl.BlockSpec((None, rows, HEAD_DIM), lambda i, h, pt, idx: (i, 0, h)),
        scratch_shapes=[
            pltpu.VMEM((2, n_sel, PAGE_SIZE, HEAD_DIM), F32),
            pltpu.VMEM((2, n_sel, PAGE_SIZE, HEAD_DIM), F32),
            pltpu.SemaphoreType.DMA((2,)),
        ],
    )
    return pl.pallas_call(
        functools.partial(_attn_sample_kernel, t=t, past=past),
        grid_spec=grid_spec,
        out_shape=jax.ShapeDtypeStruct((b, rows, WIDTH), F32),
        compiler_params=_params("arbitrary", "arbitrary"),
        name="attn_sample",
    )(page_table, idx_flat, slopes, z3, z3, z3, cache_k, cache_v)


def _outproj_kernel(x_ref, gate_ref, ret_ref, att_ref, wr_ref, wa_ref, g_ref, b_ref, o_ref):
    y = jnp.dot(ret_ref[...].astype(BF16), wr_ref[...], preferred_element_type=F32)
    y = y + jnp.dot(att_ref[...].astype(BF16), wa_ref[...], preferred_element_type=F32)
    r = ALPHA * x_ref[...] + gate_ref[...] * y
    o_ref[...] = _layernorm(r, g_ref[...], b_ref[...])


def _outproj(x, mod, ret, att, w_o, ln_g, ln_b, tm):
    rows = x.shape[0]
    return pl.pallas_call(
        _outproj_kernel,
        grid=(rows // tm,),
        in_specs=[
            pl.BlockSpec((tm, D_MODEL), lambda i: (i, 0)),
            _mod_spec(mod, tm, 1, 2),
            pl.BlockSpec((tm, WIDTH), lambda i: (i, 0)),
            pl.BlockSpec((tm, WIDTH), lambda i: (i, 0)),
            pl.BlockSpec((WIDTH, D_MODEL), lambda i: (0, 0)),
            pl.BlockSpec((WIDTH, D_MODEL), lambda i: (1, 0)),
            pl.BlockSpec((1, D_MODEL), lambda i: (0, 0)),
            pl.BlockSpec((1, D_MODEL), lambda i: (0, 0)),
        ],
        out_specs=pl.BlockSpec((tm, D_MODEL), lambda i: (i, 0)),
        out_shape=jax.ShapeDtypeStruct((rows, D_MODEL), F32),
        compiler_params=_params("parallel"),
        name="outproj",
    )(x, mod, ret, att, w_o, w_o, ln_g[1:2], ln_b[1:2])


def _row_tile(rows, want):
    return want if rows % want == 0 else rows


def kernel(x_prompt, x_sample, cache_k, cache_v, state_ret, page_table, c_prompt, c_sample,
           w_ada, b_ada, w_ffn1_in, w_ffn1_out, w_ffn2_in, w_ffn2_out, w_in, w_o, ln_g, ln_b):
    assert w_ada.shape[0] == DEPTH and x_prompt.shape[0] == 1
    seq = x_prompt.shape[1]
    bd, t, _ = x_sample.shape
    n_pages = page_table.shape[1]
    past = n_pages * PAGE_SIZE
    assert seq % MOBA_BLOCK == 0 and n_pages % PAGES_PER_BLOCK == 0
    assert n_pages // PAGES_PER_BLOCK >= MOBA_TOPK and t <= SUBLANES

    hs = jnp.arange(N_HEADS, dtype=F32)
    log_gamma = jnp.log1p(-jnp.exp2(-5.0 - hs))
    slopes = jnp.exp2(-8.0 * (hs + 1.0) / N_HEADS)

    w1i, w1o = w_ffn1_in[0].astype(BF16), w_ffn1_out[0].astype(BF16)
    w2i, w2o = w_ffn2_in[0].astype(BF16), w_ffn2_out[0].astype(BF16)
    wi, wo = w_in[0].astype(BF16), w_o[0].astype(BF16)
    g_ln, b_ln = ln_g[0], ln_b[0]

    pad = (-(1 + bd)) % SUBLANES
    c_all = jnp.concatenate([c_prompt, c_sample, jnp.zeros((pad, D_MODEL), F32)], axis=0)
    mod = _adaln(c_all, w_ada[0], b_ada[0])
    mod_p = mod[0:1]
    mod_s = jnp.repeat(mod[1:1 + bd], t, axis=0)

    xp = x_prompt[0]
    tm_p = _row_tile(seq, 512)
    tf = 512
    xp = _ffn(xp, mod_p, 0, w1i, w1o, g_ln, b_ln, tm_p, tf)
    zb, g_r, k_a, v_a, qt, vt, kmean_p = _inproj(xp, mod_p, wi, tm_p, BF16, True)
    ret, state_p = _ret_prompt(zb, g_r, log_gamma, MOBA_BLOCK)
    att, kmean_s = _moba_prompt(zb, qt, vt, kmean_p.reshape(seq // MOBA_BLOCK, WIDTH), slopes,
                                cache_k[0], page_table)
    xp = _outproj(xp, mod_p, ret, att, wo, g_ln, b_ln, tm_p)
    xp = _ffn(xp, mod_p, 2, w2i, w2o, g_ln, b_ln, tm_p, tf)

    rows_s = bd * t
    xs = x_sample.reshape(rows_s, D_MODEL)
    xs = _ffn(xs, mod_s, 0, w1i, w1o, g_ln, b_ln, rows_s, tf)
    zs, gs, ks, vs = _inproj(xs, mod_s, wi, rows_s, F32, False)
    tpad = ((0, 0), (0, SUBLANES - t), (0, 0))
    zs3 = jnp.pad(zs.reshape(bd, t, IN_WIDTH), tpad)
    gs3 = jnp.pad(gs.reshape(bd, t, WIDTH), tpad)
    ret_s, state_s = _ret_sample(zs3, gs3, state_ret[0], log_gamma, t)
    idx = _gate_sample(zs3, kmean_s)[:, :t, :N_HEADS * MOBA_TOPK].reshape(-1)
    att_s = _attn_sample(zs3, cache_k[0], cache_v[0], page_table, idx, slopes, t, past)
    xs = _outproj(xs, mod_s, ret_s[:, :t].reshape(rows_s, WIDTH),
                  att_s[:, :t].reshape(rows_s, WIDTH), wo, g_ln, b_ln, rows_s)
    xs = _ffn(xs, mod_s, 2, w2i, w2o, g_ln, b_ln, rows_s, tf)

    hd = (N_HEADS, HEAD_DIM)
    return (xp[None], xs.reshape(bd, t, D_MODEL),
            k_a.reshape(1, 1, seq, *hd), v_a.reshape(1, 1, seq, *hd), state_p[None, None],
            ks.reshape(1, bd, t, *hd), vs.reshape(1, bd, t, *hd), state_s[None])
```

```python
import functools

import jax
import jax.numpy as jnp
from jax import lax
from jax.experimental import pallas as pl
from jax.experimental.pallas import tpu as pltpu

F32 = jnp.float32
BF16 = jnp.bfloat16

D_MODEL = 2048
N_HEADS = 8
HEAD_DIM = 128
WIDTH = N_HEADS * HEAD_DIM
N_CHUNKS = 7
IN_WIDTH = N_CHUNKS * WIDTH
D_FF = 5632
MOBA_BLOCK = 256
MOBA_TOPK = 3
PAGE_SIZE = 128
PAGES_PER_BLOCK = MOBA_BLOCK // PAGE_SIZE
LN_EPS = 1e-5
DEPTH = 1
ALPHA = (2 * DEPTH) ** 0.25
FFN_RESIDUAL = 0.5
QK_SCALE = HEAD_DIM ** -0.5
NEG_INF = float("-inf")
SUBLANES = 8
BF16_SUBLANES = 16
LANES = 128
V_AUG_ROWS = HEAD_DIM + BF16_SUBLANES
MOBA_CHAINS = 2
MOBA_QTILES = 2

VMEM_LIMIT_BYTES = 56 * 1024 * 1024

CH_QR, CH_KR, CH_VR, CH_GR, CH_QA, CH_KA, CH_VA = range(N_CHUNKS)


def _params(*sem):
    return pltpu.CompilerParams(dimension_semantics=sem, vmem_limit_bytes=VMEM_LIMIT_BYTES)


def _silu(x):
    return x * jax.nn.sigmoid(x)


def _layernorm(r, g, b):
    mu = jnp.mean(r, axis=-1, keepdims=True)
    xc = r - mu
    var = jnp.mean(xc * xc, axis=-1, keepdims=True)
    return xc * lax.rsqrt(var + LN_EPS) * g + b


def _dot_nt(a, b):
    return lax.dot_general(a, b, (((1,), (1,)), ((), ())), preferred_element_type=F32)


def _dot_tn(a, b):
    return lax.dot_general(a, b, (((0,), (0,)), ((), ())), preferred_element_type=F32)


def _adaln_kernel(c_ref, w_ref, b_ref, o_ref):
    a = _silu(c_ref[...]).astype(BF16)
    o_ref[...] = jnp.dot(a, w_ref[...].astype(BF16), preferred_element_type=F32) + b_ref[...]


def _adaln(c_all, w_ada, b_ada):
    rows = c_all.shape[0]
    n = w_ada.shape[1]
    tn = 1024
    return pl.pallas_call(
        _adaln_kernel,
        grid=(n // tn,),
        in_specs=[
            pl.BlockSpec((rows, D_MODEL), lambda j: (0, 0)),
            pl.BlockSpec((D_MODEL, tn), lambda j: (0, j)),
            pl.BlockSpec((1, tn), lambda j: (0, j)),
        ],
        out_specs=pl.BlockSpec((rows, tn), lambda j: (0, j)),
        out_shape=jax.ShapeDtypeStruct((rows, n), F32),
        compiler_params=_params("arbitrary"),
        name="adaln",
    )(c_all, w_ada, b_ada.reshape(1, n))


def _mod_spec(mod, tm, sub, part):
    col = sub * 3 + part
    if mod.shape[0] == 1:
        return pl.BlockSpec((1, D_MODEL), lambda i, *_: (0, col))
    return pl.BlockSpec((tm, D_MODEL), lambda i, *_: (i, col))


def _ffn_kernel(x_ref, shift_ref, scale_ref, gate_ref, wa_ref, wu_ref, wo_ref, g_ref, b_ref,
                o_ref, xm_ref, acc_ref):
    f = pl.program_id(1)

    @pl.when(f == 0)
    def _():
        xm_ref[...] = (x_ref[...] * (1.0 + scale_ref[...]) + shift_ref[...]).astype(BF16)
        acc_ref[...] = jnp.zeros_like(acc_ref)

    xm = xm_ref[...]
    a = jnp.dot(xm, wa_ref[...], preferred_element_type=F32)
    u = jnp.dot(xm, wu_ref[...], preferred_element_type=F32)
    h = (_silu(a) * u).astype(BF16)
    acc_ref[...] += jnp.dot(h, wo_ref[...], preferred_element_type=F32)

    @pl.when(f == pl.num_programs(1) - 1)
    def _():
        r = ALPHA * x_ref[...] + FFN_RESIDUAL * gate_ref[...] * acc_ref[...]
        o_ref[...] = _layernorm(r, g_ref[...], b_ref[...])


def _ffn(x, mod, sub, w_in, w_out, ln_g, ln_b, tm, tf):
    rows = x.shape[0]
    nf = D_FF // tf
    return pl.pallas_call(
        _ffn_kernel,
        grid=(rows // tm, nf),
        in_specs=[
            pl.BlockSpec((tm, D_MODEL), lambda i, f: (i, 0)),
            _mod_spec(mod, tm, sub, 0),
            _mod_spec(mod, tm, sub, 1),
            _mod_spec(mod, tm, sub, 2),
            pl.BlockSpec((D_MODEL, tf), lambda i, f: (0, f)),
            pl.BlockSpec((D_MODEL, tf), lambda i, f: (0, f + nf)),
            pl.BlockSpec((tf, D_MODEL), lambda i, f: (f, 0)),
            pl.BlockSpec((1, D_MODEL), lambda i, f: (0, 0)),
            pl.BlockSpec((1, D_MODEL), lambda i, f: (0, 0)),
        ],
        out_specs=pl.BlockSpec((tm, D_MODEL), lambda i, f: (i, 0)),
        out_shape=jax.ShapeDtypeStruct((rows, D_MODEL), F32),
        scratch_shapes=[pltpu.VMEM((tm, D_MODEL), BF16), pltpu.VMEM((tm, D_MODEL), F32)],
        compiler_params=_params("parallel", "arbitrary"),
        name="ffn",
    )(x, mod, mod, mod, w_in, w_in, w_out, ln_g[sub:sub + 1], ln_b[sub:sub + 1])


def _inproj_kernel(x_ref, shift_ref, scale_ref, w_ref, z_ref, g_ref, k_ref, v_ref, *rest,
                   emit_transposed):
    if emit_transposed:
        qt_ref, vt_ref, km_ref, xm_ref = rest
    else:
        (xm_ref,) = rest
    j = pl.program_id(1)

    @pl.when(j == 0)
    def _():
        xm_ref[...] = (x_ref[...] * (1.0 + scale_ref[...]) + shift_ref[...]).astype(BF16)

    z = jnp.dot(xm_ref[...], w_ref[...], preferred_element_type=F32)

    def put_transposed(ref, val):
        zt = val.T.astype(BF16)
        for n in range(ref.shape[0]):
            blk = zt[:, n * MOBA_BLOCK:(n + 1) * MOBA_BLOCK]
            if ref.ndim == 3:
                ref[n] = blk
            else:
                for h in range(N_HEADS):
                    ref[n, h, :HEAD_DIM, :] = blk[h * HEAD_DIM:(h + 1) * HEAD_DIM]
                    ref[n, h, HEAD_DIM:, :] = jnp.ones((V_AUG_ROWS - HEAD_DIM, MOBA_BLOCK), BF16)

    @pl.when(j != CH_QA)
    def _():
        z_ref[...] = z.astype(z_ref.dtype)

    @pl.when(j == CH_QA)
    def _():
        zs = z * QK_SCALE
        z_ref[...] = zs.astype(z_ref.dtype)
        if emit_transposed:
            put_transposed(qt_ref, zs)

    @pl.when(j == CH_GR)
    def _():
        g_ref[...] = z

    @pl.when(j == CH_KA)
    def _():
        k_ref[...] = z
        if emit_transposed:
            n = km_ref.shape[0]
            km_ref[...] = jnp.sum(z.reshape(n, MOBA_BLOCK, WIDTH), axis=1) * (1.0 / MOBA_BLOCK)

    @pl.when(j == CH_VA)
    def _():
        v_ref[...] = z
        if emit_transposed:
            put_transposed(vt_ref, z)


def _inproj(x, mod, w_in, tm, z_dtype, emit_transposed):
    rows = x.shape[0]
    out_shape = [
        jax.ShapeDtypeStruct((rows, IN_WIDTH), z_dtype),
        jax.ShapeDtypeStruct((rows, WIDTH), F32),
        jax.ShapeDtypeStruct((rows, WIDTH), F32),
        jax.ShapeDtypeStruct((rows, WIDTH), F32),
    ]
    out_specs = [
        pl.BlockSpec((tm, WIDTH), lambda i, j: (i, j)),
        pl.BlockSpec((tm, WIDTH), lambda i, j: (i, 0)),
        pl.BlockSpec((tm, WIDTH), lambda i, j: (i, 0)),
        pl.BlockSpec((tm, WIDTH), lambda i, j: (i, 0)),
    ]
    if emit_transposed:
        nb = tm // MOBA_BLOCK
        out_shape.append(jax.ShapeDtypeStruct((rows // MOBA_BLOCK, WIDTH, MOBA_BLOCK), BF16))
        out_specs.append(pl.BlockSpec((nb, WIDTH, MOBA_BLOCK), lambda i, j: (i, 0, 0)))
        out_shape.append(jax.ShapeDtypeStruct(
            (rows // MOBA_BLOCK, N_HEADS, V_AUG_ROWS, MOBA_BLOCK), BF16))
        out_specs.append(pl.BlockSpec((nb, N_HEADS, V_AUG_ROWS, MOBA_BLOCK),
                                      lambda i, j: (i, 0, 0, 0)))
        out_shape.append(jax.ShapeDtypeStruct((rows // tm, nb, WIDTH), F32))
        out_specs.append(pl.BlockSpec((None, nb, WIDTH), lambda i, j: (i, 0, 0)))
    return pl.pallas_call(
        functools.partial(_inproj_kernel, emit_transposed=emit_transposed),
        grid=(rows // tm, N_CHUNKS),
        in_specs=[
            pl.BlockSpec((tm, D_MODEL), lambda i, j: (i, 0)),
            _mod_spec(mod, tm, 1, 0),
            _mod_spec(mod, tm, 1, 1),
            pl.BlockSpec((D_MODEL, WIDTH), lambda i, j: (0, j)),
        ],
        out_specs=out_specs,
        out_shape=out_shape,
        scratch_shapes=[pltpu.VMEM((tm, D_MODEL), BF16)],
        compiler_params=_params("parallel", "arbitrary"),
        name="inproj",
    )(x, mod, mod, w_in)


def _decay_tables(lg, c, n):
    row = lax.broadcasted_iota(jnp.int32, (n, n), 0)
    col = lax.broadcasted_iota(jnp.int32, (n, n), 1)
    diff = (row - col).astype(F32)
    dmat = jnp.where(diff >= 0, jnp.exp(lg * jnp.maximum(diff, 0.0)), 0.0) * QK_SCALE
    i = lax.broadcasted_iota(jnp.int32, (n, HEAD_DIM), 0).astype(F32)
    cross = jnp.exp(lg * (i + 1.0))
    wtail = jnp.exp(lg * (c - 1.0 - i)) * QK_SCALE
    return dmat, cross, wtail


def _groupnorm_gate(o, g):
    mu = jnp.mean(o, axis=-1, keepdims=True)
    oc = o - mu
    var = jnp.mean(oc * oc, axis=-1, keepdims=True)
    return _silu(g) * (oc * lax.rsqrt(var + LN_EPS))


def _retention_heads(heads):
    first = []
    for q, k, kf, v, state, _, _, wtail, _ in heads:
        first.append((_dot_nt(q, k),
                      jnp.dot(q, state.astype(BF16), preferred_element_type=F32),
                      _dot_tn((kf * wtail).astype(BF16), v)))
    out = []
    for (_, _, _, v, state, dmat, cross, _, gc), (s, o_cross, kv) in zip(heads, first):
        o = jnp.dot((s * dmat).astype(BF16), v, preferred_element_type=F32) + o_cross * cross
        out.append((o, gc * state + kv))
    return out


def _ret_prompt_kernel(lg_ref, q_ref, k_ref, v_ref, g_ref, o_ref, state_ref,
                       dmat_ref, cross_ref, wtail_ref, *, chunk):
    c = pl.program_id(0)

    @pl.when(c == 0)
    def _():
        for h in range(N_HEADS):
            dmat_ref[h], cross_ref[h], wtail_ref[h] = _decay_tables(lg_ref[h], chunk, chunk)
        state_ref[...] = jnp.zeros_like(state_ref)

    heads = []
    for h in range(N_HEADS):
        cols = slice(h * HEAD_DIM, (h + 1) * HEAD_DIM)
        k = k_ref[:, cols]
        gc = jnp.exp(jnp.full((HEAD_DIM, HEAD_DIM), lg_ref[h] * chunk, F32))
        heads.append((q_ref[:, cols], k, k.astype(F32), v_ref[:, cols], state_ref[h],
                      dmat_ref[h], cross_ref[h], wtail_ref[h], gc))
    for h, (o, state) in enumerate(_retention_heads(heads)):
        cols = slice(h * HEAD_DIM, (h + 1) * HEAD_DIM)
        state_ref[h] = state
        o_ref[:, cols] = _groupnorm_gate(o, g_ref[:, cols]).astype(o_ref.dtype)


def _ret_prompt(zb, g, log_gamma, chunk):
    s = zb.shape[0]
    return pl.pallas_call(
        functools.partial(_ret_prompt_kernel, chunk=chunk),
        grid=(s // chunk,),
        in_specs=[
            pl.BlockSpec(memory_space=pltpu.SMEM),
            pl.BlockSpec((chunk, WIDTH), lambda c: (c, CH_QR)),
            pl.BlockSpec((chunk, WIDTH), lambda c: (c, CH_KR)),
            pl.BlockSpec((chunk, WIDTH), lambda c: (c, CH_VR)),
            pl.BlockSpec((chunk, WIDTH), lambda c: (c, 0)),
        ],
        out_specs=[
            pl.BlockSpec((chunk, WIDTH), lambda c: (c, 0)),
            pl.BlockSpec((N_HEADS, HEAD_DIM, HEAD_DIM), lambda c: (0, 0, 0)),
        ],
        out_shape=[
            jax.ShapeDtypeStruct((s, WIDTH), BF16),
            jax.ShapeDtypeStruct((N_HEADS, HEAD_DIM, HEAD_DIM), F32),
        ],
        scratch_shapes=[
            pltpu.VMEM((N_HEADS, chunk, chunk), F32),
            pltpu.VMEM((N_HEADS, chunk, HEAD_DIM), F32),
            pltpu.VMEM((N_HEADS, chunk, HEAD_DIM), F32),
        ],
        compiler_params=_params("arbitrary"),
        name="ret_prompt",
    )(log_gamma, zb, zb, zb, g)


def _ret_sample_kernel(lg_ref, z_ref, g_ref, st_ref, o_ref, sn_ref, *, t):
    n = z_ref.shape[0]
    heads = []
    for h in range(N_HEADS):
        lg = lg_ref[h]
        dmat, cross, wtail = _decay_tables(lg, t, n)

        def head(ch, h=h):
            c0 = (ch * N_HEADS + h) * HEAD_DIM
            return z_ref[:, c0:c0 + HEAD_DIM]

        kf = head(CH_KR)
        gc = jnp.exp(jnp.full((HEAD_DIM, HEAD_DIM), lg * t, F32))
        heads.append((head(CH_QR).astype(BF16), kf.astype(BF16), kf, head(CH_VR).astype(BF16),
                      st_ref[h], dmat, cross, wtail, gc))
    for h, (o, state) in enumerate(_retention_heads(heads)):
        sn_ref[h] = state
        gh = g_ref[:, h * HEAD_DIM:(h + 1) * HEAD_DIM]
        o_ref[:, h * HEAD_DIM:(h + 1) * HEAD_DIM] = _groupnorm_gate(o, gh)


def _ret_sample(z3, g3, state, log_gamma, t):
    b, n, _ = z3.shape
    return pl.pallas_call(
        functools.partial(_ret_sample_kernel, t=t),
        grid=(b,),
        in_specs=[
            pl.BlockSpec(memory_space=pltpu.SMEM),
            pl.BlockSpec((None, n, IN_WIDTH), lambda i: (i, 0, 0)),
            pl.BlockSpec((None, n, WIDTH), lambda i: (i, 0, 0)),
            pl.BlockSpec((None, N_HEADS, HEAD_DIM, HEAD_DIM), lambda i: (i, 0, 0, 0)),
        ],
        out_specs=[
            pl.BlockSpec((None, n, WIDTH), lambda i: (i, 0, 0)),
            pl.BlockSpec((None, N_HEADS, HEAD_DIM, HEAD_DIM), lambda i: (i, 0, 0, 0)),
        ],
        out_shape=[
            jax.ShapeDtypeStruct((b, n, WIDTH), F32),
            jax.ShapeDtypeStruct(state.shape, F32),
        ],
        compiler_params=_params("parallel"),
        name="ret_sample",
    )(log_gamma, z3, g3, state)


def _top_mask_t(gate_t, n_valid, top):
    n = gate_t.shape[0]
    blk = lax.broadcasted_iota(jnp.int32, gate_t.shape, 0)
    g = jnp.where(blk < n_valid, gate_t, NEG_INF)
    sel = jnp.zeros(gate_t.shape, F32)
    for _ in range(top):
        m = jnp.max(g, axis=0, keepdims=True)
        first = jnp.min(jnp.where(g == m, blk, n), axis=0, keepdims=True)
        hit = blk == first
        finite = jnp.abs(m) < float("inf")
        sel = jnp.where(hit, jnp.where(finite, 1.0, sel), sel)
        g = jnp.where(hit, NEG_INF, g)
    return sel


def _moba_prompt_kernel(pt_ref, slope_ref, qt_ref, k_ref, vt_ref, km_ref, cache_hbm,
                        o_ref, kms_ref, sel_ref, kaug_ref, pbuf, psem, *bufs):
    u_refs, acc_refs = bufs[:2 * MOBA_CHAINS], bufs[2 * MOBA_CHAINS:]
    h = pl.program_id(0)
    pair = pl.program_id(1)
    slope = slope_ref[h]
    blk = MOBA_BLOCK
    nb = sel_ref.shape[0]
    qw = MOBA_QTILES * blk
    tiles = [pair * MOBA_QTILES + t for t in range(MOBA_QTILES)]
    i_col = pair * MOBA_QTILES + lax.broadcasted_iota(jnp.int32, (1, qw), 1) // blk
    i_last = tiles[-1]

    step = h * pl.num_programs(1) + pair
    slot = lax.rem(step, 2)
    pages_per_step = pbuf.shape[1]
    n_pages = pt_ref.shape[1]

    def page_copies(st, sl):
        first = st * pages_per_step
        seq = first // n_pages
        page0 = first - seq * n_pages
        return [pltpu.make_async_copy(cache_hbm.at[pt_ref[seq, page0 + r]], pbuf.at[sl, r],
                                      psem.at[sl]) for r in range(pages_per_step)]

    @pl.when(step == 0)
    def _():
        for cp in page_copies(step, slot):
            cp.start()

    @pl.when(step + 1 < pl.num_programs(0) * pl.num_programs(1))
    def _():
        for cp in page_copies(step + 1, 1 - slot):
            cp.start()

    @pl.when(pair == 0)
    def _():
        kc = lax.broadcasted_iota(jnp.int32, (blk, HEAD_DIM), 0)
        col = lax.broadcasted_iota(jnp.int32, (blk, HEAD_DIM), 1)
        kaug_ref[...] = jnp.where(col == 0, slope * kc.astype(F32), 0.0).astype(BF16)

    for cp in page_copies(step, slot):
        cp.wait()
    for n in range(pages_per_step // PAGES_PER_BLOCK):
        tot = None
        for r in range(PAGES_PER_BLOCK):
            part = jnp.sum(pbuf[slot, n * PAGES_PER_BLOCK + r], axis=0)
            tot = part if tot is None else tot + part
        kms_ref[n] = tot * (1.0 / MOBA_BLOCK)

    qt = jnp.concatenate([qt_ref[t] for t in range(MOBA_QTILES)], axis=1)
    gate_t = jnp.dot(km_ref[...].astype(BF16), qt, preferred_element_type=F32)
    sel_ref[...] = _top_mask_t(gate_t, i_col, MOBA_TOPK)

    one_row = lax.broadcasted_iota(jnp.int32, (HEAD_DIM, qw), 0) == 0
    qt_aug = jnp.concatenate([qt, jnp.where(one_row, 1.0, 0.0).astype(BF16)], axis=0)
    kaug = kaug_ref[...]

    def scores_t(j, q_aug=qt_aug):
        js = pl.multiple_of(j * blk, blk)
        kj = jnp.concatenate([k_ref[pl.ds(js, blk), :], kaug], axis=1)
        return jnp.dot(kj, q_aug, preferred_element_type=F32)

    def values_t(j, p):
        r = jnp.dot(vt_ref[j], p.astype(BF16), preferred_element_type=F32)
        return r[HEAD_DIM:HEAD_DIM + 1], r[:HEAD_DIM]

    def block_of(r, c):
        return jnp.minimum(r * MOBA_CHAINS + c, nb - 1)

    def issue_scores(r, buf_set):
        maxes = []
        for c in range(MOBA_CHAINS):
            u = scores_t(block_of(r, c))
            u_refs[buf_set * MOBA_CHAINS + c][...] = u
            maxes.append(jnp.max(u, axis=0, keepdims=True))
        return maxes

    u0 = [scores_t(tiles[t], qt_aug[:, t * blk:(t + 1) * blk]) for t in range(MOBA_QTILES)]
    mx0 = issue_scores(0, 0)
    kc = lax.broadcasted_iota(jnp.int32, (blk, blk), 0)
    qr = lax.broadcasted_iota(jnp.int32, (blk, blk), 1)
    own = []
    for t in range(MOBA_QTILES):
        u = jnp.where(kc <= qr, u0[t], NEG_INF)
        m = jnp.max(u, axis=0, keepdims=True)
        own.append((m, jnp.exp(u - m)))
    own = [(m,) + values_t(tiles[t], p) for t, (m, p) in enumerate(own)]
    m0, l0, acc0 = (jnp.concatenate([o[k] for o in own], axis=1) for k in range(3))
    acc_refs[0][...] = acc0
    for c in range(1, MOBA_CHAINS):
        acc_refs[c][...] = jnp.zeros((HEAD_DIM, qw), F32)

    def chain_step(r, buf_set, chain, m, l, u_max):
        j_raw = r * MOBA_CHAINS + chain
        j = block_of(r, chain)
        picked = jnp.where(j_raw < i_col, sel_ref[pl.ds(j, 1), :], 0.0) > 0.0
        off = -slope * ((i_col - j) * blk).astype(F32)
        u_ref, acc_ref = u_refs[buf_set * MOBA_CHAINS + chain], acc_refs[chain]
        m_new = jnp.where(picked, jnp.maximum(m, u_max + off), m)
        x = u_ref[...] - jnp.where(picked, m_new - off, float("inf"))
        p = jnp.exp(x.astype(BF16))
        alpha = jnp.where(m_new == NEG_INF, 1.0, jnp.exp(m - m_new))
        lj, accj = values_t(j, p)
        acc_ref[...] = alpha * acc_ref[...] + accj
        return m_new, alpha * l + lj

    def body(rr, carry):
        ms, ls, mx = (list(x) for x in carry)
        for buf_set in range(2):
            r = 2 * rr + buf_set
            mx_next = issue_scores(r + 1, 1 - buf_set)
            for c in range(MOBA_CHAINS):
                ms[c], ls[c] = chain_step(r, buf_set, c, ms[c], ls[c], mx[c])
            mx = mx_next
        return tuple(ms), tuple(ls), tuple(mx)

    ms0 = (m0,) + (jnp.full((1, qw), NEG_INF, F32),) * (MOBA_CHAINS - 1)
    ls0 = (l0,) + (jnp.zeros((1, qw), F32),) * (MOBA_CHAINS - 1)
    per_iter = 2 * MOBA_CHAINS
    n_iter = lax.div(i_last + per_iter - 1, per_iter)
    ms, ls, _ = lax.fori_loop(0, n_iter, body, (ms0, ls0, tuple(mx0)))

    m_all = functools.reduce(jnp.maximum, ms)
    l = jnp.zeros((1, qw), F32)
    acc = jnp.zeros((HEAD_DIM, qw), F32)
    for c in range(MOBA_CHAINS):
        w = jnp.where(ms[c] == NEG_INF, 0.0, jnp.exp(ms[c] - m_all))
        l = l + w * ls[c]
        acc = acc + w * acc_refs[c][...]
    o_ref[...] = (acc / l).T.astype(o_ref.dtype)


def _moba_prompt(zb, qt, vt, kmean, slopes, cache_k, page_table):
    s = zb.shape[0]
    nb = s // MOBA_BLOCK
    assert nb % MOBA_QTILES == 0
    n_pairs = nb // MOBA_QTILES
    qw = MOBA_QTILES * MOBA_BLOCK
    n_seq, n_pages = page_table.shape
    n_steps = N_HEADS * n_pairs
    pps = (n_seq * n_pages) // n_steps
    assert pps * n_steps == n_seq * n_pages and pps % PAGES_PER_BLOCK == 0 and n_pages % pps == 0
    bps = pps // PAGES_PER_BLOCK

    def kms_index(h, i, pt):
        first = (h * n_pairs + i) * pps
        return (first // n_pages, (first % n_pages) // pps, 0, 0)

    grid_spec = pltpu.PrefetchScalarGridSpec(
        num_scalar_prefetch=1,
        grid=(N_HEADS, n_pairs),
        in_specs=[
            pl.BlockSpec(memory_space=pltpu.SMEM),
            pl.BlockSpec((MOBA_QTILES, HEAD_DIM, MOBA_BLOCK), lambda h, i, pt: (i, h, 0)),
            pl.BlockSpec((s, HEAD_DIM), lambda h, i, pt: (0, CH_KA * N_HEADS + h)),
            pl.BlockSpec((nb, None, V_AUG_ROWS, MOBA_BLOCK), lambda h, i, pt: (0, h, 0, 0)),
            pl.BlockSpec((nb, HEAD_DIM), lambda h, i, pt: (0, h)),
            pl.BlockSpec(memory_space=pl.ANY),
        ],
        out_specs=[
            pl.BlockSpec((qw, HEAD_DIM), lambda h, i, pt: (i, h)),
            pl.BlockSpec((None, bps, N_HEADS, HEAD_DIM), kms_index),
        ],
        scratch_shapes=[
            pltpu.VMEM((nb, qw), F32),
            pltpu.VMEM((MOBA_BLOCK, HEAD_DIM), BF16),
            pltpu.VMEM((2, pps, PAGE_SIZE, N_HEADS, HEAD_DIM), F32),
            pltpu.SemaphoreType.DMA((2,)),
        ] + [pltpu.VMEM((MOBA_BLOCK, qw), F32)] * (2 * MOBA_CHAINS)
          + [pltpu.VMEM((HEAD_DIM, qw), F32)] * MOBA_CHAINS,
    )
    return pl.pallas_call(
        _moba_prompt_kernel,
        grid_spec=grid_spec,
        out_shape=[
            jax.ShapeDtypeStruct((s, WIDTH), BF16),
            jax.ShapeDtypeStruct((n_seq, n_pages // PAGES_PER_BLOCK, N_HEADS, HEAD_DIM), F32),
        ],
        compiler_params=_params("arbitrary", "arbitrary"),
        name="moba_prompt",
    )(page_table, slopes, qt, zb, vt, kmean, cache_k)


def _gate_sample_kernel(z_ref, km_ref, o_ref):
    rows = z_ref.shape[0]
    n = km_ref.shape[0]
    lane = lax.broadcasted_iota(jnp.int32, (rows, LANES), 1)
    blk = lax.broadcasted_iota(jnp.int32, (rows, n), 1)
    out = jnp.zeros((rows, LANES), jnp.int32)
    for h in range(N_HEADS):
        c0 = (CH_QA * N_HEADS + h) * HEAD_DIM
        q = z_ref[:, c0:c0 + HEAD_DIM].astype(BF16)
        kmh = km_ref[:, h, :].astype(BF16)
        g = _dot_nt(q, kmh)
        for r in range(MOBA_TOPK):
            m = jnp.max(g, axis=1, keepdims=True)
            first = jnp.min(jnp.where(g == m, blk, n), axis=1, keepdims=True)
            out = jnp.where(lane == h * MOBA_TOPK + r, first, out)
            g = jnp.where(blk == first, NEG_INF, g)
    o_ref[...] = out


def _gate_sample(z3, kmean_s):
    b, rows, _ = z3.shape
    n = kmean_s.shape[1]
    return pl.pallas_call(
        _gate_sample_kernel,
        grid=(b,),
        in_specs=[
            pl.BlockSpec((None, rows, IN_WIDTH), lambda i: (i, 0, 0)),
            pl.BlockSpec((None, n, N_HEADS, HEAD_DIM), lambda i: (i, 0, 0, 0)),
        ],
        out_specs=pl.BlockSpec((None, rows, LANES), lambda i: (i, 0, 0)),
        out_shape=jax.ShapeDtypeStruct((b, rows, LANES), jnp.int32),
        compiler_params=_params("parallel"),
        name="gate_sample",
    )(z3, kmean_s)


def _attn_sample_kernel(pt_ref, idx_ref, slope_ref, q_ref, kn_ref, vn_ref, ck_hbm, cv_hbm, o_ref,
                        kbuf, vbuf, sem, *, t, past):
    per_tok = MOBA_TOPK * PAGES_PER_BLOCK
    b = pl.program_id(0)
    h = pl.program_id(1)
    step = b * N_HEADS + h
    slot = lax.rem(step, 2)

    def copies(bb, hh, sl):
        out = []
        for tt in range(t):
            for r in range(MOBA_TOPK):
                blk = idx_ref[((bb * t + tt) * N_HEADS + hh) * MOBA_TOPK + r]
                for pg in range(PAGES_PER_BLOCK):
                    phys = pt_ref[bb, blk * PAGES_PER_BLOCK + pg]
                    j = tt * per_tok + r * PAGES_PER_BLOCK + pg
                    out.append(pltpu.make_async_copy(ck_hbm.at[phys, :, hh, :], kbuf.at[sl, j],
                                                     sem.at[sl]))
                    out.append(pltpu.make_async_copy(cv_hbm.at[phys, :, hh, :], vbuf.at[sl, j],
                                                     sem.at[sl]))
        return out

    @pl.when(step == 0)
    def _():
        for cp in copies(b, h, slot):
            cp.start()

    @pl.when(step + 1 < pl.num_programs(0) * N_HEADS)
    def _():
        last_head = h == N_HEADS - 1
        for cp in copies(jnp.where(last_head, b + 1, b), jnp.where(last_head, 0, h + 1), 1 - slot):
            cp.start()

    for cp in copies(b, h, slot):
        cp.wait()

    k_refs = [kbuf.at[slot, j] for j in range(t * per_tok)]
    v_refs = [vbuf.at[slot, j] for j in range(t * per_tok)]
    slope = slope_ref[h]
    rows = q_ref.shape[0]
    q = q_ref[...].astype(BF16)
    zpad = jnp.zeros((LANES - rows, HEAD_DIM), F32)
    kn = jnp.concatenate([kn_ref[...], zpad], axis=0)
    vn = jnp.concatenate([vn_ref[...], zpad], axis=0)
    lane_blk = lax.broadcasted_iota(jnp.int32, (1, MOBA_BLOCK), 1)
    lane_new = lax.broadcasted_iota(jnp.int32, (1, LANES), 1)
    scores = []
    for tt in range(t):
        ks = jnp.concatenate([k_refs[tt * per_tok + r][...] for r in range(per_tok)] + [kn], axis=0)
        scores.append(_dot_nt(q, ks.astype(BF16))[tt:tt + 1])
    probs = []
    for tt in range(t):
        dist = []
        for r in range(MOBA_TOPK):
            blk = idx_ref[((b * t + tt) * N_HEADS + h) * MOBA_TOPK + r]
            dist.append((past + tt - blk * MOBA_BLOCK - lane_blk).astype(F32))
        dist.append((tt - lane_new).astype(F32))
        valid = jnp.concatenate([jnp.ones((1, MOBA_TOPK * MOBA_BLOCK), jnp.int32),
                                 (lane_new <= tt).astype(jnp.int32)], axis=1)
        s = jnp.where(valid > 0, scores[tt] - slope * jnp.concatenate(dist, axis=1), NEG_INF)
        p = jnp.exp(s - jnp.max(s, axis=1, keepdims=True))
        probs.append((p, jnp.sum(p, axis=1, keepdims=True)))
    for tt in range(t):
        p, l = probs[tt]
        vs = jnp.concatenate([v_refs[tt * per_tok + r][...] for r in range(per_tok)] + [vn], axis=0)
        p8 = jnp.broadcast_to(p, (SUBLANES, p.shape[1])).astype(BF16)
        o = jnp.dot(p8, vs.astype(BF16), preferred_element_type=F32)[0:1]
        o_ref[tt:tt + 1, :] = o / l
    if rows > t:
        o_ref[t:, :] = jnp.zeros((rows - t, HEAD_DIM), F32)


def _attn_sample(z3, cache_k, cache_v, page_table, idx_flat, slopes, t, past):
    b, rows, _ = z3.shape
    n_sel = t * MOBA_TOPK * PAGES_PER_BLOCK

    def tok_spec(ch):
        return pl.BlockSpec((None, rows, HEAD_DIM), lambda i, h, pt, idx: (i, 0, ch * N_HEADS + h))

    grid_spec = pltpu.PrefetchScalarGridSpec(
        num_scalar_prefetch=2,
        grid=(b, N_HEADS),
        in_specs=[pl.BlockSpec(memory_space=pltpu.SMEM), tok_spec(CH_QA), tok_spec(CH_KA),
                  tok_spec(CH_VA), pl.BlockSpec(memory_space=pl.ANY),
                  pl.BlockSpec(memory_space=pl.ANY)],
        out_specs=pl.BlockSpec((None, rows, HEAD_DIM), lambda i, h, pt, idx: (i, 0, h)),
        scratch_shapes=[
            pltpu.VMEM((2, n_sel, PAGE_SIZE, HEAD_DIM), F32),
            pltpu.VMEM((2, n_sel, PAGE_SIZE, HEAD_DIM), F32),
            pltpu.SemaphoreType.DMA((2,)),
        ],
    )
    return pl.pallas_call(
        functools.partial(_attn_sample_kernel, t=t, past=past),
        grid_spec=grid_spec,
        out_shape=jax.ShapeDtypeStruct((b, rows, WIDTH), F32),
        compiler_params=_params("arbitrary", "arbitrary"),
        name="attn_sample",
    )(page_table, idx_flat, slopes, z3, z3, z3, cache_k, cache_v)


def _outproj_kernel(x_ref, gate_ref, ret_ref, att_ref, wr_ref, wa_ref, g_ref, b_ref, o_ref):
    y = jnp.dot(ret_ref[...].astype(BF16), wr_ref[...], preferred_element_type=F32)
    y = y + jnp.dot(att_ref[...].astype(BF16), wa_ref[...], preferred_element_type=F32)
    r = ALPHA * x_ref[...] + gate_ref[...] * y
    o_ref[...] = _layernorm(r, g_ref[...], b_ref[...])


def _outproj(x, mod, ret, att, w_o, ln_g, ln_b, tm):
    rows = x.shape[0]
    return pl.pallas_call(
        _outproj_kernel,
        grid=(rows // tm,),
        in_specs=[
            pl.BlockSpec((tm, D_MODEL), lambda i: (i, 0)),
            _mod_spec(mod, tm, 1, 2),
            pl.BlockSpec((tm, WIDTH), lambda i: (i, 0)),
            pl.BlockSpec((tm, WIDTH), lambda i: (i, 0)),
            pl.BlockSpec((WIDTH, D_MODEL), lambda i: (0, 0)),
            pl.BlockSpec((WIDTH, D_MODEL), lambda i: (1, 0)),
            pl.BlockSpec((1, D_MODEL), lambda i: (0, 0)),
            pl.BlockSpec((1, D_MODEL), lambda i: (0, 0)),
        ],
        out_specs=pl.BlockSpec((tm, D_MODEL), lambda i: (i, 0)),
        out_shape=jax.ShapeDtypeStruct((rows, D_MODEL), F32),
        compiler_params=_params("parallel"),
        name="outproj",
    )(x, mod, ret, att, w_o, w_o, ln_g[1:2], ln_b[1:2])


def _row_tile(rows, want):
    return want if rows % want == 0 else rows


def kernel(x_prompt, x_sample, cache_k, cache_v, state_ret, page_table, c_prompt, c_sample,
           w_ada, b_ada, w_ffn1_in, w_ffn1_out, w_ffn2_in, w_ffn2_out, w_in, w_o, ln_g, ln_b):
    assert w_ada.shape[0] == DEPTH and x_prompt.shape[0] == 1
    seq = x_prompt.shape[1]
    bd, t, _ = x_sample.shape
    n_pages = page_table.shape[1]
    past = n_pages * PAGE_SIZE
    assert seq % MOBA_BLOCK == 0 and n_pages % PAGES_PER_BLOCK == 0
    assert n_pages // PAGES_PER_BLOCK >= MOBA_TOPK and t <= SUBLANES

    hs = jnp.arange(N_HEADS, dtype=F32)
    log_gamma = jnp.log1p(-jnp.exp2(-5.0 - hs))
    slopes = jnp.exp2(-8.0 * (hs + 1.0) / N_HEADS)

    w1i, w1o = w_ffn1_in[0].astype(BF16), w_ffn1_out[0].astype(BF16)
    w2i, w2o = w_ffn2_in[0].astype(BF16), w_ffn2_out[0].astype(BF16)
    wi, wo = w_in[0].astype(BF16), w_o[0].astype(BF16)
    g_ln, b_ln = ln_g[0], ln_b[0]

    pad = (-(1 + bd)) % SUBLANES
    c_all = jnp.concatenate([c_prompt, c_sample, jnp.zeros((pad, D_MODEL), F32)], axis=0)
    mod = _adaln(c_all, w_ada[0], b_ada[0])
    mod_p = mod[0:1]
    mod_s = jnp.repeat(mod[1:1 + bd], t, axis=0)

    xp = x_prompt[0]
    tm_p = _row_tile(seq, 512)
    tf = 512
    xp = _ffn(xp, mod_p, 0, w1i, w1o, g_ln, b_ln, tm_p, tf)
    zb, g_r, k_a, v_a, qt, vt, kmean_p = _inproj(xp, mod_p, wi, tm_p, BF16, True)
    ret, state_p = _ret_prompt(zb, g_r, log_gamma, MOBA_BLOCK)
    att, kmean_s = _moba_prompt(zb, qt, vt, kmean_p.reshape(seq // MOBA_BLOCK, WIDTH), slopes,
                                cache_k[0], page_table)
    xp = _outproj(xp, mod_p, ret, att, wo, g_ln, b_ln, tm_p)
    xp = _ffn(xp, mod_p, 2, w2i, w2o, g_ln, b_ln, tm_p, tf)

    rows_s = bd * t
    xs = x_sample.reshape(rows_s, D_MODEL)
    xs = _ffn(xs, mod_s, 0, w1i, w1o, g_ln, b_ln, rows_s, tf)
    zs, gs, ks, vs = _inproj(xs, mod_s, wi, rows_s, F32, False)
    tpad = ((0, 0), (0, SUBLANES - t), (0, 0))
    zs3 = jnp.pad(zs.reshape(bd, t, IN_WIDTH), tpad)
    gs3 = jnp.pad(gs.reshape(bd, t, WIDTH), tpad)
    ret_s, state_s = _ret_sample(zs3, gs3, state_ret[0], log_gamma, t)
    idx = _gate_sample(zs3, kmean_s)[:, :t, :N_HEADS * MOBA_TOPK].reshape(-1)
    att_s = _attn_sample(zs3, cache_k[0], cache_v[0], page_table, idx, slopes, t, past)
    xs = _outproj(xs, mod_s, ret_s[:, :t].reshape(rows_s, WIDTH),
                  att_s[:, :t].reshape(rows_s, WIDTH), wo, g_ln, b_ln, rows_s)
    xs = _ffn(xs, mod_s, 2, w2i, w2o, g_ln, b_ln, rows_s, tf)

    hd = (N_HEADS, HEAD_DIM)
    return (xp[None], xs.reshape(bd, t, D_MODEL),
            k_a.reshape(1, 1, seq, *hd), v_a.reshape(1, 1, seq, *hd), state_p[None, None],
            ks.reshape(1, bd, t, *hd), vs.reshape(1, bd, t, *hd), state_s[None])
```

```python
import functools

import jax
import jax.numpy as jnp
from jax import lax
from jax.experimental import pallas as pl
from jax.experimental.pallas import tpu as pltpu

F32 = jnp.float32
BF16 = jnp.bfloat16

D_MODEL = 2048
N_HEADS = 8
HEAD_DIM = 128
WIDTH = N_HEADS * HEAD_DIM
N_CHUNKS = 7
IN_WIDTH = N_CHUNKS * WIDTH
D_FF = 5632
MOBA_BLOCK = 256
MOBA_TOPK = 3
PAGE_SIZE = 128
PAGES_PER_BLOCK = MOBA_BLOCK // PAGE_SIZE
LN_EPS = 1e-5
DEPTH = 1
ALPHA = (2 * DEPTH) ** 0.25
FFN_RESIDUAL = 0.5
QK_SCALE = HEAD_DIM ** -0.5
NEG_INF = float("-inf")
SUBLANES = 8
BF16_SUBLANES = 16
LANES = 128
V_AUG_ROWS = HEAD_DIM + BF16_SUBLANES
MOBA_CHAINS = 2
MOBA_QTILES = 2

VMEM_LIMIT_BYTES = 56 * 1024 * 1024

CH_QR, CH_KR, CH_VR, CH_GR, CH_QA, CH_KA, CH_VA = range(N_CHUNKS)


def _params(*sem):
    return pltpu.CompilerParams(dimension_semantics=sem, vmem_limit_bytes=VMEM_LIMIT_BYTES)


def _silu(x):
    return x * jax.nn.sigmoid(x)


def _layernorm(r, g, b):
    mu = jnp.mean(r, axis=-1, keepdims=True)
    xc = r - mu
    var = jnp.mean(xc * xc, axis=-1, keepdims=True)
    return xc * lax.rsqrt(var + LN_EPS) * g + b


def _dot_nt(a, b):
    return lax.dot_general(a, b, (((1,), (1,)), ((), ())), preferred_element_type=F32)


def _dot_tn(a, b):
    return lax.dot_general(a, b, (((0,), (0,)), ((), ())), preferred_element_type=F32)


def _adaln_kernel(c_ref, w_ref, b_ref, o_ref):
    a = _silu(c_ref[...]).astype(BF16)
    o_ref[...] = jnp.dot(a, w_ref[...].astype(BF16), preferred_element_type=F32) + b_ref[...]


def _adaln(c_all, w_ada, b_ada):
    rows = c_all.shape[0]
    n = w_ada.shape[1]
    tn = 1024
    return pl.pallas_call(
        _adaln_kernel,
        grid=(n // tn,),
        in_specs=[
            pl.BlockSpec((rows, D_MODEL), lambda j: (0, 0)),
            pl.BlockSpec((D_MODEL, tn), lambda j: (0, j)),
            pl.BlockSpec((1, tn), lambda j: (0, j)),
        ],
        out_specs=pl.BlockSpec((rows, tn), lambda j: (0, j)),
        out_shape=jax.ShapeDtypeStruct((rows, n), F32),
        compiler_params=_params("arbitrary"),
        name="adaln",
    )(c_all, w_ada, b_ada.reshape(1, n))


def _mod_spec(mod, tm, sub, part):
    col = sub * 3 + part
    if mod.shape[0] == 1:
        return pl.BlockSpec((1, D_MODEL), lambda i, *_: (0, col))
    return pl.BlockSpec((tm, D_MODEL), lambda i, *_: (i, col))


def _ffn_kernel(x_ref, shift_ref, scale_ref, gate_ref, wa_ref, wu_ref, wo_ref, g_ref, b_ref,
                o_ref, xm_ref, acc_ref):
    f = pl.program_id(1)

    @pl.when(f == 0)
    def _():
        xm_ref[...] = (x_ref[...] * (1.0 + scale_ref[...]) + shift_ref[...]).astype(BF16)
        acc_ref[...] = jnp.zeros_like(acc_ref)

    xm = xm_ref[...]
    a = jnp.dot(xm, wa_ref[...], preferred_element_type=F32)
    u = jnp.dot(xm, wu_ref[...], preferred_element_type=F32)
    h = (_silu(a) * u).astype(BF16)
    acc_ref[...] += jnp.dot(h, wo_ref[...], preferred_element_type=F32)

    @pl.when(f == pl.num_programs(1) - 1)
    def _():
        r = ALPHA * x_ref[...] + FFN_RESIDUAL * gate_ref[...] * acc_ref[...]
        o_ref[...] = _layernorm(r, g_ref[...], b_ref[...])


def _ffn(x, mod, sub, w_in, w_out, ln_g, ln_b, tm, tf):
    rows = x.shape[0]
    nf = D_FF // tf
    return pl.pallas_call(
        _ffn_kernel,
        grid=(rows // tm, nf),
        in_specs=[
            pl.BlockSpec((tm, D_MODEL), lambda i, f: (i, 0)),
            _mod_spec(mod, tm, sub, 0),
            _mod_spec(mod, tm, sub, 1),
            _mod_spec(mod, tm, sub, 2),
            pl.BlockSpec((D_MODEL, tf), lambda i, f: (0, f)),
            pl.BlockSpec((D_MODEL, tf), lambda i, f: (0, f + nf)),
            pl.BlockSpec((tf, D_MODEL), lambda i, f: (f, 0)),
            pl.BlockSpec((1, D_MODEL), lambda i, f: (0, 0)),
            pl.BlockSpec((1, D_MODEL), lambda i, f: (0, 0)),
        ],
        out_specs=pl.BlockSpec((tm, D_MODEL), lambda i, f: (i, 0)),
        out_shape=jax.ShapeDtypeStruct((rows, D_MODEL), F32),
        scratch_shapes=[pltpu.VMEM((tm, D_MODEL), BF16), pltpu.VMEM((tm, D_MODEL), F32)],
        compiler_params=_params("parallel", "arbitrary"),
        name="ffn",
    )(x, mod, mod, mod, w_in, w_in, w_out, ln_g[sub:sub + 1], ln_b[sub:sub + 1])


def _inproj_kernel(x_ref, shift_ref, scale_ref, w_ref, z_ref, g_ref, k_ref, v_ref, *rest,
                   emit_transposed):
    if emit_transposed:
        qt_ref, vt_ref, km_ref, xm_ref = rest
    else:
        (xm_ref,) = rest
    j = pl.program_id(1)

    @pl.when(j == 0)
    def _():
        xm_ref[...] = (x_ref[...] * (1.0 + scale_ref[...]) + shift_ref[...]).astype(BF16)

    z = jnp.dot(xm_ref[...], w_ref[...], preferred_element_type=F32)

    def put_transposed(ref, val):
        zt = val.T.astype(BF16)
        for n in range(ref.shape[0]):
            blk = zt[:, n * MOBA_BLOCK:(n + 1) * MOBA_BLOCK]
            if ref.ndim == 3:
                ref[n] = blk
            else:
                for h in range(N_HEADS):
                    ref[n, h, :HEAD_DIM, :] = blk[h * HEAD_DIM:(h + 1) * HEAD_DIM]
                    ref[n, h, HEAD_DIM:, :] = jnp.ones((V_AUG_ROWS - HEAD_DIM, MOBA_BLOCK), BF16)

    @pl.when(j != CH_QA)
    def _():
        z_ref[...] = z.astype(z_ref.dtype)

    @pl.when(j == CH_QA)
    def _():
        zs = z * QK_SCALE
        z_ref[...] = zs.astype(z_ref.dtype)
        if emit_transposed:
            put_transposed(qt_ref, zs)

    @pl.when(j == CH_GR)
    def _():
        g_ref[...] = z

    @pl.when(j == CH_KA)
    def _():
        k_ref[...] = z
        if emit_transposed:
            n = km_ref.shape[0]
            km_ref[...] = jnp.sum(z.reshape(n, MOBA_BLOCK, WIDTH), axis=1) * (1.0 / MOBA_BLOCK)

    @pl.when(j == CH_VA)
    def _():
        v_ref[...] = z
        if emit_transposed:
            put_transposed(vt_ref, z)


def _inproj(x, mod, w_in, tm, z_dtype, emit_transposed):
    rows = x.shape[0]
    out_shape = [
        jax.ShapeDtypeStruct((rows, IN_WIDTH), z_dtype),
        jax.ShapeDtypeStruct((rows, WIDTH), F32),
        jax.ShapeDtypeStruct((rows, WIDTH), F32),
        jax.ShapeDtypeStruct((rows, WIDTH), F32),
    ]
    out_specs = [
        pl.BlockSpec((tm, WIDTH), lambda i, j: (i, j)),
        pl.BlockSpec((tm, WIDTH), lambda i, j: (i, 0)),
        pl.BlockSpec((tm, WIDTH), lambda i, j: (i, 0)),
        pl.BlockSpec((tm, WIDTH), lambda i, j: (i, 0)),
    ]
    if emit_transposed:
        nb = tm // MOBA_BLOCK
        out_shape.append(jax.ShapeDtypeStruct((rows // MOBA_BLOCK, WIDTH, MOBA_BLOCK), BF16))
        out_specs.append(pl.BlockSpec((nb, WIDTH, MOBA_BLOCK), lambda i, j: (i, 0, 0)))
        out_shape.append(jax.ShapeDtypeStruct(
            (rows // MOBA_BLOCK, N_HEADS, V_AUG_ROWS, MOBA_BLOCK), BF16))
        out_specs.append(pl.BlockSpec((nb, N_HEADS, V_AUG_ROWS, MOBA_BLOCK),
                                      lambda i, j: (i, 0, 0, 0)))
        out_shape.append(jax.ShapeDtypeStruct((rows // tm, nb, WIDTH), F32))
        out_specs.append(pl.BlockSpec((None, nb, WIDTH), lambda i, j: (i, 0, 0)))
    return pl.pallas_call(
        functools.partial(_inproj_kernel, emit_transposed=emit_transposed),
        grid=(rows // tm, N_CHUNKS),
        in_specs=[
            pl.BlockSpec((tm, D_MODEL), lambda i, j: (i, 0)),
            _mod_spec(mod, tm, 1, 0),
            _mod_spec(mod, tm, 1, 1),
            pl.BlockSpec((D_MODEL, WIDTH), lambda i, j: (0, j)),
        ],
        out_specs=out_specs,
        out_shape=out_shape,
        scratch_shapes=[pltpu.VMEM((tm, D_MODEL), BF16)],
        compiler_params=_params("parallel", "arbitrary"),
        name="inproj",
    )(x, mod, mod, w_in)


def _decay_tables(lg, c, n):
    row = lax.broadcasted_iota(jnp.int32, (n, n), 0)
    col = lax.broadcasted_iota(jnp.int32, (n, n), 1)
    diff = (row - col).astype(F32)
    dmat = jnp.where(diff >= 0, jnp.exp(lg * jnp.maximum(diff, 0.0)), 0.0) * QK_SCALE
    i = lax.broadcasted_iota(jnp.int32, (n, HEAD_DIM), 0).astype(F32)
    cross = jnp.exp(lg * (i + 1.0))
    wtail = jnp.exp(lg * (c - 1.0 - i)) * QK_SCALE
    return dmat, cross, wtail


def _groupnorm_gate(o, g):
    mu = jnp.mean(o, axis=-1, keepdims=True)
    oc = o - mu
    var = jnp.mean(oc * oc, axis=-1, keepdims=True)
    return _silu(g) * (oc * lax.rsqrt(var + LN_EPS))


def _retention_heads(heads):
    first = []
    for q, k, kf, v, state, _, _, wtail, _ in heads:
        first.append((_dot_nt(q, k),
                      jnp.dot(q, state.astype(BF16), preferred_element_type=F32),
                      _dot_tn((kf * wtail).astype(BF16), v)))
    out = []
    for (_, _, _, v, state, dmat, cross, _, gc), (s, o_cross, kv) in zip(heads, first):
        o = jnp.dot((s * dmat).astype(BF16), v, preferred_element_type=F32) + o_cross * cross
        out.append((o, gc * state + kv))
    return out


def _ret_prompt_kernel(lg_ref, q_ref, k_ref, v_ref, g_ref, o_ref, state_ref,
                       dmat_ref, cross_ref, wtail_ref, *, chunk):
    c = pl.program_id(0)

    @pl.when(c == 0)
    def _():
        for h in range(N_HEADS):
            dmat_ref[h], cross_ref[h], wtail_ref[h] = _decay_tables(lg_ref[h], chunk, chunk)
        state_ref[...] = jnp.zeros_like(state_ref)

    heads = []
    for h in range(N_HEADS):
        cols = slice(h * HEAD_DIM, (h + 1) * HEAD_DIM)
        k = k_ref[:, cols]
        gc = jnp.exp(jnp.full((HEAD_DIM, HEAD_DIM), lg_ref[h] * chunk, F32))
        heads.append((q_ref[:, cols], k, k.astype(F32), v_ref[:, cols], state_ref[h],
                      dmat_ref[h], cross_ref[h], wtail_ref[h], gc))
    for h, (o, state) in enumerate(_retention_heads(heads)):
        cols = slice(h * HEAD_DIM, (h + 1) * HEAD_DIM)
        state_ref[h] = state
        o_ref[:, cols] = _groupnorm_gate(o, g_ref[:, cols]).astype(o_ref.dtype)


def _ret_prompt(zb, g, log_gamma, chunk):
    s = zb.shape[0]
    return pl.pallas_call(
        functools.partial(_ret_prompt_kernel, chunk=chunk),
        grid=(s // chunk,),
        in_specs=[
            pl.BlockSpec(memory_space=pltpu.SMEM),
            pl.BlockSpec((chunk, WIDTH), lambda c: (c, CH_QR)),
            pl.BlockSpec((chunk, WIDTH), lambda c: (c, CH_KR)),
            pl.BlockSpec((chunk, WIDTH), lambda c: (c, CH_VR)),
            pl.BlockSpec((chunk, WIDTH), lambda c: (c, 0)),
        ],
        out_specs=[
            pl.BlockSpec((chunk, WIDTH), lambda c: (c, 0)),
            pl.BlockSpec((N_HEADS, HEAD_DIM, HEAD_DIM), lambda c: (0, 0, 0)),
        ],
        out_shape=[
            jax.ShapeDtypeStruct((s, WIDTH), BF16),
            jax.ShapeDtypeStruct((N_HEADS, HEAD_DIM, HEAD_DIM), F32),
        ],
        scratch_shapes=[
            pltpu.VMEM((N_HEADS, chunk, chunk), F32),
            pltpu.VMEM((N_HEADS, chunk, HEAD_DIM), F32),
            pltpu.VMEM((N_HEADS, chunk, HEAD_DIM), F32),
        ],
        compiler_params=_params("arbitrary"),
        name="ret_prompt",
    )(log_gamma, zb, zb, zb, g)


def _ret_sample_kernel(lg_ref, z_ref, g_ref, st_ref, o_ref, sn_ref, *, t):
    n = z_ref.shape[0]
    heads = []
    for h in range(N_HEADS):
        lg = lg_ref[h]
        dmat, cross, wtail = _decay_tables(lg, t, n)

        def head(ch, h=h):
            c0 = (ch * N_HEADS + h) * HEAD_DIM
            return z_ref[:, c0:c0 + HEAD_DIM]

        kf = head(CH_KR)
        gc = jnp.exp(jnp.full((HEAD_DIM, HEAD_DIM), lg * t, F32))
        heads.append((head(CH_QR).astype(BF16), kf.astype(BF16), kf, head(CH_VR).astype(BF16),
                      st_ref[h], dmat, cross, wtail, gc))
    for h, (o, state) in enumerate(_retention_heads(heads)):
        sn_ref[h] = state
        gh = g_ref[:, h * HEAD_DIM:(h + 1) * HEAD_DIM]
        o_ref[:, h * HEAD_DIM:(h + 1) * HEAD_DIM] = _groupnorm_gate(o, gh)


def _ret_sample(z3, g3, state, log_gamma, t):
    b, n, _ = z3.shape
    return pl.pallas_call(
        functools.partial(_ret_sample_kernel, t=t),
        grid=(b,),
        in_specs=[
            pl.BlockSpec(memory_space=pltpu.SMEM),
            pl.BlockSpec((None, n, IN_WIDTH), lambda i: (i, 0, 0)),
            pl.BlockSpec((None, n, WIDTH), lambda i: (i, 0, 0)),
            pl.BlockSpec((None, N_HEADS, HEAD_DIM, HEAD_DIM), lambda i: (i, 0, 0, 0)),
        ],
        out_specs=[
            pl.BlockSpec((None, n, WIDTH), lambda i: (i, 0, 0)),
            pl.BlockSpec((None, N_HEADS, HEAD_DIM, HEAD_DIM), lambda i: (i, 0, 0, 0)),
        ],
        out_shape=[
            jax.ShapeDtypeStruct((b, n, WIDTH), F32),
            jax.ShapeDtypeStruct(state.shape, F32),
        ],
        compiler_params=_params("parallel"),
        name="ret_sample",
    )(log_gamma, z3, g3, state)


def _top_mask_t(gate_t, n_valid, top):
    n = gate_t.shape[0]
    blk = lax.broadcasted_iota(jnp.int32, gate_t.shape, 0)
    g = jnp.where(blk < n_valid, gate_t, NEG_INF)
    sel = jnp.zeros(gate_t.shape, F32)
    for _ in range(top):
        m = jnp.max(g, axis=0, keepdims=True)
        first = jnp.min(jnp.where(g == m, blk, n), axis=0, keepdims=True)
        hit = blk == first
        finite = jnp.abs(m) < float("inf")
        sel = jnp.where(hit, jnp.where(finite, 1.0, sel), sel)
        g = jnp.where(hit, NEG_INF, g)
    return sel


def _moba_prompt_kernel(pt_ref, slope_ref, qt_ref, k_ref, vt_ref, km_ref, cache_hbm,
                        o_ref, kms_ref, sel_ref, kaug_ref, pbuf, psem, *bufs):
    u_refs, acc_refs = bufs[:2 * MOBA_CHAINS], bufs[2 * MOBA_CHAINS:]
    h = pl.program_id(0)
    pair = pl.program_id(1)
    slope = slope_ref[h]
    blk = MOBA_BLOCK
    nb = sel_ref.shape[0]
    qw = MOBA_QTILES * blk
    tiles = [pair * MOBA_QTILES + t for t in range(MOBA_QTILES)]
    i_col = pair * MOBA_QTILES + lax.broadcasted_iota(jnp.int32, (1, qw), 1) // blk
    i_last = tiles[-1]

    step = h * pl.num_programs(1) + pair
    slot = lax.rem(step, 2)
    pages_per_step = pbuf.shape[1]
    n_pages = pt_ref.shape[1]

    def page_copies(st, sl):
        first = st * pages_per_step
        seq = first // n_pages
        page0 = first - seq * n_pages
        return [pltpu.make_async_copy(cache_hbm.at[pt_ref[seq, page0 + r]], pbuf.at[sl, r],
                                      psem.at[sl]) for r in range(pages_per_step)]

    @pl.when(step == 0)
    def _():
        for cp in page_copies(step, slot):
            cp.start()

    @pl.when(step + 1 < pl.num_programs(0) * pl.num_programs(1))
    def _():
        for cp in page_copies(step + 1, 1 - slot):
            cp.start()

    @pl.when(pair == 0)
    def _():
        kc = lax.broadcasted_iota(jnp.int32, (blk, HEAD_DIM), 0)
        col = lax.broadcasted_iota(jnp.int32, (blk, HEAD_DIM), 1)
        kaug_ref[...] = jnp.where(col == 0, slope * kc.astype(F32), 0.0).astype(BF16)

    for cp in page_copies(step, slot):
        cp.wait()
    for n in range(pages_per_step // PAGES_PER_BLOCK):
        tot = None
        for r in range(PAGES_PER_BLOCK):
            part = jnp.sum(pbuf[slot, n * PAGES_PER_BLOCK + r], axis=0)
            tot = part if tot is None else tot + part
        kms_ref[n] = tot * (1.0 / MOBA_BLOCK)

    qt = jnp.concatenate([qt_ref[t] for t in range(MOBA_QTILES)], axis=1)
    gate_t = jnp.dot(km_ref[...].astype(BF16), qt, preferred_element_type=F32)
    sel_ref[...] = _top_mask_t(gate_t, i_col, MOBA_TOPK)

    one_row = lax.broadcasted_iota(jnp.int32, (HEAD_DIM, qw), 0) == 0
    qt_aug = jnp.concatenate([qt, jnp.where(one_row, 1.0, 0.0).astype(BF16)], axis=0)
    kaug = kaug_ref[...]

    def scores_t(j, q_aug=qt_aug):
        js = pl.multiple_of(j * blk, blk)
        kj = jnp.concatenate([k_ref[pl.ds(js, blk), :], kaug], axis=1)
        return jnp.dot(kj, q_aug, preferred_element_type=F32)

    def values_t(j, p):
        r = jnp.dot(vt_ref[j], p.astype(BF16), preferred_element_type=F32)
        return r[HEAD_DIM:HEAD_DIM + 1], r[:HEAD_DIM]

    def block_of(r, c):
        return jnp.minimum(r * MOBA_CHAINS + c, nb - 1)

    def issue_scores(r, buf_set):
        maxes = []
        for c in range(MOBA_CHAINS):
            u = scores_t(block_of(r, c))
            u_refs[buf_set * MOBA_CHAINS + c][...] = u
            maxes.append(jnp.max(u, axis=0, keepdims=True))
        return maxes

    u0 = [scores_t(tiles[t], qt_aug[:, t * blk:(t + 1) * blk]) for t in range(MOBA_QTILES)]
    mx0 = issue_scores(0, 0)
    kc = lax.broadcasted_iota(jnp.int32, (blk, blk), 0)
    qr = lax.broadcasted_iota(jnp.int32, (blk, blk), 1)
    own = []
    for t in range(MOBA_QTILES):
        u = jnp.where(kc <= qr, u0[t], NEG_INF)
        m = jnp.max(u, axis=0, keepdims=True)
        own.append((m, jnp.exp(u - m)))
    own = [(m,) + values_t(tiles[t], p) for t, (m, p) in enumerate(own)]
    m0, l0, acc0 = (jnp.concatenate([o[k] for o in own], axis=1) for k in range(3))
    acc_refs[0][...] = acc0
    for c in range(1, MOBA_CHAINS):
        acc_refs[c][...] = jnp.zeros((HEAD_DIM, qw), F32)

    def chain_step(r, buf_set, chain, m, l, u_max):
        j_raw = r * MOBA_CHAINS + chain
        j = block_of(r, chain)
        picked = jnp.where(j_raw < i_col, sel_ref[pl.ds(j, 1), :], 0.0) > 0.0
        off = -slope * ((i_col - j) * blk).astype(F32)
        u_ref, acc_ref = u_refs[buf_set * MOBA_CHAINS + chain], acc_refs[chain]
        m_new = jnp.where(picked, jnp.maximum(m, u_max + off), m)
        x = u_ref[...] - jnp.where(picked, m_new - off, float("inf"))
        p = jnp.exp(x.astype(BF16))
        alpha = jnp.where(m_new == NEG_INF, 1.0, jnp.exp(m - m_new))
        lj, accj = values_t(j, p)
        acc_ref[...] = alpha * acc_ref[...] + accj
        return m_new, alpha * l + lj

    def body(rr, carry):
        ms, ls, mx = (list(x) for x in carry)
        for buf_set in range(2):
            r = 2 * rr + buf_set
            mx_next = issue_scores(r + 1, 1 - buf_set)
            for c in range(MOBA_CHAINS):
                ms[c], ls[c] = chain_step(r, buf_set, c, ms[c], ls[c], mx[c])
            mx = mx_next
        return tuple(ms), tuple(ls), tuple(mx)

    ms0 = (m0,) + (jnp.full((1, qw), NEG_INF, F32),) * (MOBA_CHAINS - 1)
    ls0 = (l0,) + (jnp.zeros((1, qw), F32),) * (MOBA_CHAINS - 1)
    per_iter = 2 * MOBA_CHAINS
    n_iter = lax.div(i_last + per_iter - 1, per_iter)
    ms, ls, _ = lax.fori_loop(0, n_iter, body, (ms0, ls0, tuple(mx0)))

    m_all = functools.reduce(jnp.maximum, ms)
    l = jnp.zeros((1, qw), F32)
    acc = jnp.zeros((HEAD_DIM, qw), F32)
    for c in range(MOBA_CHAINS):
        w = jnp.where(ms[c] == NEG_INF, 0.0, jnp.exp(ms[c] - m_all))
        l = l + w * ls[c]
        acc = acc + w * acc_refs[c][...]
    o_ref[...] = (acc / l).T.astype(o_ref.dtype)


def _moba_prompt(zb, qt, vt, kmean, slopes, cache_k, page_table):
    s = zb.shape[0]
    nb = s // MOBA_BLOCK
    assert nb % MOBA_QTILES == 0
    n_pairs = nb // MOBA_QTILES
    qw = MOBA_QTILES * MOBA_BLOCK
    n_seq, n_pages = page_table.shape
    n_steps = N_HEADS * n_pairs
    pps = (n_seq * n_pages) // n_steps
    assert pps * n_steps == n_seq * n_pages and pps % PAGES_PER_BLOCK == 0 and n_pages % pps == 0
    bps = pps // PAGES_PER_BLOCK

    def kms_index(h, i, pt):
        first = (h * n_pairs + i) * pps
        return (first // n_pages, (first % n_pages) // pps, 0, 0)

    grid_spec = pltpu.PrefetchScalarGridSpec(
        num_scalar_prefetch=1,
        grid=(N_HEADS, n_pairs),
        in_specs=[
            pl.BlockSpec(memory_space=pltpu.SMEM),
            pl.BlockSpec((MOBA_QTILES, HEAD_DIM, MOBA_BLOCK), lambda h, i, pt: (i, h, 0)),
            pl.BlockSpec((s, HEAD_DIM), lambda h, i, pt: (0, CH_KA * N_HEADS + h)),
            pl.BlockSpec((nb, None, V_AUG_ROWS, MOBA_BLOCK), lambda h, i, pt: (0, h, 0, 0)),
            pl.BlockSpec((nb, HEAD_DIM), lambda h, i, pt: (0, h)),
            pl.BlockSpec(memory_space=pl.ANY),
        ],
        out_specs=[
            pl.BlockSpec((qw, HEAD_DIM), lambda h, i, pt: (i, h)),
            pl.BlockSpec((None, bps, N_HEADS, HEAD_DIM), kms_index),
        ],
        scratch_shapes=[
            pltpu.VMEM((nb, qw), F32),
            pltpu.VMEM((MOBA_BLOCK, HEAD_DIM), BF16),
            pltpu.VMEM((2, pps, PAGE_SIZE, N_HEADS, HEAD_DIM), F32),
            pltpu.SemaphoreType.DMA((2,)),
        ] + [pltpu.VMEM((MOBA_BLOCK, qw), F32)] * (2 * MOBA_CHAINS)
          + [pltpu.VMEM((HEAD_DIM, qw), F32)] * MOBA_CHAINS,
    )
    return pl.pallas_call(
        _moba_prompt_kernel,
        grid_spec=grid_spec,
        out_shape=[
            jax.ShapeDtypeStruct((s, WIDTH), BF16),
            jax.ShapeDtypeStruct((n_seq, n_pages // PAGES_PER_BLOCK, N_HEADS, HEAD_DIM), F32),
        ],
        compiler_params=_params("arbitrary", "arbitrary"),
        name="moba_prompt",
    )(page_table, slopes, qt, zb, vt, kmean, cache_k)


def _gate_sample_kernel(z_ref, km_ref, o_ref):
    rows = z_ref.shape[0]
    n = km_ref.shape[0]
    lane = lax.broadcasted_iota(jnp.int32, (rows, LANES), 1)
    blk = lax.broadcasted_iota(jnp.int32, (rows, n), 1)
    out = jnp.zeros((rows, LANES), jnp.int32)
    for h in range(N_HEADS):
        c0 = (CH_QA * N_HEADS + h) * HEAD_DIM
        q = z_ref[:, c0:c0 + HEAD_DIM].astype(BF16)
        kmh = km_ref[:, h, :].astype(BF16)
        g = _dot_nt(q, kmh)
        for r in range(MOBA_TOPK):
            m = jnp.max(g, axis=1, keepdims=True)
            first = jnp.min(jnp.where(g == m, blk, n), axis=1, keepdims=True)
            out = jnp.where(lane == h * MOBA_TOPK + r, first, out)
            g = jnp.where(blk == first, NEG_INF, g)
    o_ref[...] = out


def _gate_sample(z3, kmean_s):
    b, rows, _ = z3.shape
    n = kmean_s.shape[1]
    return pl.pallas_call(
        _gate_sample_kernel,
        grid=(b,),
        in_specs=[
            pl.BlockSpec((None, rows, IN_WIDTH), lambda i: (i, 0, 0)),
            pl.BlockSpec((None, n, N_HEADS, HEAD_DIM), lambda i: (i, 0, 0, 0)),
        ],
        out_specs=pl.BlockSpec((None, rows, LANES), lambda i: (i, 0, 0)),
        out_shape=jax.ShapeDtypeStruct((b, rows, LANES), jnp.int32),
        compiler_params=_params("parallel"),
        name="gate_sample",
    )(z3, kmean_s)


def _attn_sample_kernel(pt_ref, idx_ref, slope_ref, q_ref, kn_ref, vn_ref, ck_hbm, cv_hbm, o_ref,
                        kbuf, vbuf, sem, *, t, past):
    per_tok = MOBA_TOPK * PAGES_PER_BLOCK
    b = pl.program_id(0)
    h = pl.program_id(1)
    step = b * N_HEADS + h
    slot = lax.rem(step, 2)

    def copies(bb, hh, sl):
        out = []
        for tt in range(t):
            for r in range(MOBA_TOPK):
                blk = idx_ref[((bb * t + tt) * N_HEADS + hh) * MOBA_TOPK + r]
                for pg in range(PAGES_PER_BLOCK):
                    phys = pt_ref[bb, blk * PAGES_PER_BLOCK + pg]
                    j = tt * per_tok + r * PAGES_PER_BLOCK + pg
                    out.append(pltpu.make_async_copy(ck_hbm.at[phys, :, hh, :], kbuf.at[sl, j],
                                                     sem.at[sl]))
                    out.append(pltpu.make_async_copy(cv_hbm.at[phys, :, hh, :], vbuf.at[sl, j],
                                                     sem.at[sl]))
        return out

    def start_all(cps):
        for n, cp in enumerate(cps):
            cp.start(priority=n % 2)

    @pl.when(step == 0)
    def _():
        start_all(copies(b, h, slot))

    @pl.when(step + 1 < pl.num_programs(0) * N_HEADS)
    def _():
        last_head = h == N_HEADS - 1
        start_all(copies(jnp.where(last_head, b + 1, b), jnp.where(last_head, 0, h + 1), 1 - slot))

    for cp in copies(b, h, slot):
        cp.wait()

    k_refs = [kbuf.at[slot, j] for j in range(t * per_tok)]
    v_refs = [vbuf.at[slot, j] for j in range(t * per_tok)]
    slope = slope_ref[h]
    rows = q_ref.shape[0]
    q = q_ref[...].astype(BF16)
    zpad = jnp.zeros((LANES - rows, HEAD_DIM), F32)
    kn = jnp.concatenate([kn_ref[...], zpad], axis=0)
    vn = jnp.concatenate([vn_ref[...], zpad], axis=0)
    lane_blk = lax.broadcasted_iota(jnp.int32, (1, MOBA_BLOCK), 1)
    lane_new = lax.broadcasted_iota(jnp.int32, (1, LANES), 1)
    scores = []
    for tt in range(t):
        ks = jnp.concatenate([k_refs[tt * per_tok + r][...] for r in range(per_tok)] + [kn], axis=0)
        scores.append(_dot_nt(q, ks.astype(BF16))[tt:tt + 1])
    probs = []
    for tt in range(t):
        dist = []
        for r in range(MOBA_TOPK):
            blk = idx_ref[((b * t + tt) * N_HEADS + h) * MOBA_TOPK + r]
            dist.append((past + tt - blk * MOBA_BLOCK - lane_blk).astype(F32))
        dist.append((tt - lane_new).astype(F32))
        valid = jnp.concatenate([jnp.ones((1, MOBA_TOPK * MOBA_BLOCK), jnp.int32),
                                 (lane_new <= tt).astype(jnp.int32)], axis=1)
        s = jnp.where(valid > 0, scores[tt] - slope * jnp.concatenate(dist, axis=1), NEG_INF)
        p = jnp.exp(s - jnp.max(s, axis=1, keepdims=True))
        probs.append((p, jnp.sum(p, axis=1, keepdims=True)))
    for tt in range(t):
        p, l = probs[tt]
        vs = jnp.concatenate([v_refs[tt * per_tok + r][...] for r in range(per_tok)] + [vn], axis=0)
        p8 = jnp.broadcast_to(p, (SUBLANES, p.shape[1])).astype(BF16)
        o = jnp.dot(p8, vs.astype(BF16), preferred_element_type=F32)[0:1]
        o_ref[tt:tt + 1, :] = o / l
    if rows > t:
        o_ref[t:, :] = jnp.zeros((rows - t, HEAD_DIM), F32)


def _attn_sample(z3, cache_k, cache_v, page_table, idx_flat, slopes, t, past):
    b, rows, _ = z3.shape
    n_sel = t * MOBA_TOPK * PAGES_PER_BLOCK

    def tok_spec(ch):
        return pl.BlockSpec((None, rows, HEAD_DIM), lambda i, h, pt, idx: (i, 0, ch * N_HEADS + h))

    grid_spec = pltpu.PrefetchScalarGridSpec(
        num_scalar_prefetch=2,
        grid=(b, N_HEADS),
        in_specs=[pl.BlockSpec(memory_space=pltpu.SMEM), tok_spec(CH_QA), tok_spec(CH_KA),
                  tok_spec(CH_VA), pl.BlockSpec(memory_space=pl.ANY),
                  pl.BlockSpec(memory_space=pl.ANY)],
        out_specs=pl.BlockSpec((None, rows, HEAD_DIM), lambda i, h, pt, idx: (i, 0, h)),
        scratch_shapes=[
            pltpu.VMEM((2, n_sel, PAGE_SIZE, HEAD_DIM), F32),
            pltpu.VMEM((2, n_sel, PAGE_SIZE, HEAD_DIM), F32),
            pltpu.SemaphoreType.DMA((2,)),
        ],
    )
    return pl.pallas_call(
        functools.partial(_attn_sample_kernel, t=t, past=past),
        grid_spec=grid_spec,
        out_shape=jax.ShapeDtypeStruct((b, rows, WIDTH), F32),
        compiler_params=_params("arbitrary", "arbitrary"),
        name="attn_sample",
    )(page_table, idx_flat, slopes, z3, z3, z3, cache_k, cache_v)


def _outproj_kernel(x_ref, gate_ref, ret_ref, att_ref, wr_ref, wa_ref, g_ref, b_ref, o_ref):
    y = jnp.dot(ret_ref[...].astype(BF16), wr_ref[...], preferred_element_type=F32)
    y = y + jnp.dot(att_ref[...].astype(BF16), wa_ref[...], preferred_element_type=F32)
    r = ALPHA * x_ref[...] + gate_ref[...] * y
    o_ref[...] = _layernorm(r, g_ref[...], b_ref[...])


def _outproj(x, mod, ret, att, w_o, ln_g, ln_b, tm):
    rows = x.shape[0]
    return pl.pallas_call(
        _outproj_kernel,
        grid=(rows // tm,),
        in_specs=[
            pl.BlockSpec((tm, D_MODEL), lambda i: (i, 0)),
            _mod_spec(mod, tm, 1, 2),
            pl.BlockSpec((tm, WIDTH), lambda i: (i, 0)),
            pl.BlockSpec((tm, WIDTH), lambda i: (i, 0)),
            pl.BlockSpec((WIDTH, D_MODEL), lambda i: (0, 0)),
            pl.BlockSpec((WIDTH, D_MODEL), lambda i: (1, 0)),
            pl.BlockSpec((1, D_MODEL), lambda i: (0, 0)),
            pl.BlockSpec((1, D_MODEL), lambda i: (0, 0)),
        ],
        out_specs=pl.BlockSpec((tm, D_MODEL), lambda i: (i, 0)),
        out_shape=jax.ShapeDtypeStruct((rows, D_MODEL), F32),
        compiler_params=_params("parallel"),
        name="outproj",
    )(x, mod, ret, att, w_o, w_o, ln_g[1:2], ln_b[1:2])


def _row_tile(rows, want):
    return want if rows % want == 0 else rows


def kernel(x_prompt, x_sample, cache_k, cache_v, state_ret, page_table, c_prompt, c_sample,
           w_ada, b_ada, w_ffn1_in, w_ffn1_out, w_ffn2_in, w_ffn2_out, w_in, w_o, ln_g, ln_b):
    assert w_ada.shape[0] == DEPTH and x_prompt.shape[0] == 1
    seq = x_prompt.shape[1]
    bd, t, _ = x_sample.shape
    n_pages = page_table.shape[1]
    past = n_pages * PAGE_SIZE
    assert seq % MOBA_BLOCK == 0 and n_pages % PAGES_PER_BLOCK == 0
    assert n_pages // PAGES_PER_BLOCK >= MOBA_TOPK and t <= SUBLANES

    hs = jnp.arange(N_HEADS, dtype=F32)
    log_gamma = jnp.log1p(-jnp.exp2(-5.0 - hs))
    slopes = jnp.exp2(-8.0 * (hs + 1.0) / N_HEADS)

    w1i, w1o = w_ffn1_in[0].astype(BF16), w_ffn1_out[0].astype(BF16)
    w2i, w2o = w_ffn2_in[0].astype(BF16), w_ffn2_out[0].astype(BF16)
    wi, wo = w_in[0].astype(BF16), w_o[0].astype(BF16)
    g_ln, b_ln = ln_g[0], ln_b[0]

    pad = (-(1 + bd)) % SUBLANES
    c_all = jnp.concatenate([c_prompt, c_sample, jnp.zeros((pad, D_MODEL), F32)], axis=0)
    mod = _adaln(c_all, w_ada[0], b_ada[0])
    mod_p = mod[0:1]
    mod_s = jnp.repeat(mod[1:1 + bd], t, axis=0)

    xp = x_prompt[0]
    tm_p = _row_tile(seq, 512)
    tf = 512
    xp = _ffn(xp, mod_p, 0, w1i, w1o, g_ln, b_ln, tm_p, tf)
    zb, g_r, k_a, v_a, qt, vt, kmean_p = _inproj(xp, mod_p, wi, tm_p, BF16, True)
    ret, state_p = _ret_prompt(zb, g_r, log_gamma, MOBA_BLOCK)
    att, kmean_s = _moba_prompt(zb, qt, vt, kmean_p.reshape(seq // MOBA_BLOCK, WIDTH), slopes,
                                cache_k[0], page_table)
    xp = _outproj(xp, mod_p, ret, att, wo, g_ln, b_ln, tm_p)
    xp = _ffn(xp, mod_p, 2, w2i, w2o, g_ln, b_ln, tm_p, tf)

    rows_s = bd * t
    xs = x_sample.reshape(rows_s, D_MODEL)
    xs = _ffn(xs, mod_s, 0, w1i, w1o, g_ln, b_ln, rows_s, tf)
    zs, gs, ks, vs = _inproj(xs, mod_s, wi, rows_s, F32, False)
    tpad = ((0, 0), (0, SUBLANES - t), (0, 0))
    zs3 = jnp.pad(zs.reshape(bd, t, IN_WIDTH), tpad)
    gs3 = jnp.pad(gs.reshape(bd, t, WIDTH), tpad)
    ret_s, state_s = _ret_sample(zs3, gs3, state_ret[0], log_gamma, t)
    idx = _gate_sample(zs3, kmean_s)[:, :t, :N_HEADS * MOBA_TOPK].reshape(-1)
    att_s = _attn_sample(zs3, cache_k[0], cache_v[0], page_table, idx, slopes, t, past)
    xs = _outproj(xs, mod_s, ret_s[:, :t].reshape(rows_s, WIDTH),
                  att_s[:, :t].reshape(rows_s, WIDTH), wo, g_ln, b_ln, rows_s)
    xs = _ffn(xs, mod_s, 2, w2i, w2o, g_ln, b_ln, rows_s, tf)

    hd = (N_HEADS, HEAD_DIM)
    return (xp[None], xs.reshape(bd, t, D_MODEL),
            k_a.reshape(1, 1, seq, *hd), v_a.reshape(1, 1, seq, *hd), state_p[None, None],
            ks.reshape(1, bd, t, *hd), vs.reshape(1, bd, t, *hd), state_s[None])
```
